```python
import jax, jax.numpy as jnp
from jax import lax
import numpy as np

D_MODEL = 2048
BATCH = 2
SEQ = 8192
DEPTH = 4

D_MIX = D_MODEL
D_SGU = D_MIX // 2
SGU_GROUPS = 8
SGU_GROUP_DIM = D_SGU // SGU_GROUPS
SGU_CHUNK = 128
D_DN = D_MIX - D_SGU
DN_HEADS = 8
DN_HEAD_DIM = D_DN // DN_HEADS
DN_CHUNK = 64
CONV_WIDTH = 5
NORM_EPS = 1e-6
IN_SIZES = (D_SGU, D_SGU, D_SGU, 3 * D_DN, D_DN, DN_HEADS, DN_HEADS, DN_HEADS, DN_HEADS)
D_IN = 3 * D_SGU + 4 * D_DN + 4 * DN_HEADS

kernel_name = "hybrid_gmlp_gated_deltanet_encoder"


def _rmsnorm(x, w):
    x32 = x.astype(jnp.float32)
    y = x32 * lax.rsqrt(jnp.mean(x32 * x32, axis=-1, keepdims=True) + NORM_EPS)
    return (y * w.astype(jnp.float32)).astype(x.dtype)


def _layernorm(x, g, b):
    x32 = x.astype(jnp.float32)
    mu = jnp.mean(x32, axis=-1, keepdims=True)
    xc = x32 - mu
    var = jnp.mean(xc * xc, axis=-1, keepdims=True)
    y = xc * lax.rsqrt(var + NORM_EPS)
    return (y * g.astype(jnp.float32) + b.astype(jnp.float32)).astype(x.dtype)


def _l2norm(x):
    return x * lax.rsqrt(jnp.sum(x * x, axis=-1, keepdims=True) + NORM_EPS)


def _split_in(proj):
    offsets = [int(o) for o in np.cumsum(IN_SIZES)[:-1]]
    return jnp.split(proj, offsets, axis=-1)


def _short_conv(x, w):
    c = x.shape[-1]
    pad = CONV_WIDTH // 2
    return lax.conv_general_dilated(
        x, w[:, None, :].astype(x.dtype), window_strides=(1,),
        padding=((pad, pad),), dimension_numbers=("NWC", "WIO", "NWC"),
        feature_group_count=c)


def _gated_delta_chunked(q, k, v, g, beta):
    b_, h, s, dk = q.shape
    dv = v.shape[-1]
    nc = s // DN_CHUNK
    q = q.reshape(b_, h, nc, DN_CHUNK, dk)
    k = k.reshape(b_, h, nc, DN_CHUNK, dk)
    v = v.reshape(b_, h, nc, DN_CHUNK, dv)
    g_cum = jnp.cumsum(g.reshape(b_, h, nc, DN_CHUNK), axis=-1)
    beta = beta.reshape(b_, h, nc, DN_CHUNK)

    lower = jnp.tril(jnp.ones((DN_CHUNK, DN_CHUNK), dtype=bool))
    strict = jnp.tril(jnp.ones((DN_CHUNK, DN_CHUNK), dtype=bool), k=-1)
    diff = g_cum[..., :, None] - g_cum[..., None, :]
    decay = jnp.where(lower, jnp.exp(jnp.where(lower, diff, 0.0)), 0.0)

    k_beta = k * beta[..., None]
    m = jnp.where(strict, jnp.einsum("bhnid,bhnjd->bhnij", k_beta, k) * decay, 0.0)
    eye = jnp.eye(DN_CHUNK, dtype=q.dtype)
    t_inv = lax.linalg.triangular_solve(
        eye + m, jnp.broadcast_to(eye, m.shape), left_side=True, lower=True,
        unit_diagonal=True)
    u = jnp.einsum("bhnij,bhnjd->bhnid", t_inv, v * beta[..., None])
    w = jnp.einsum("bhnij,bhnjd->bhnid", t_inv, k_beta * jnp.exp(g_cum)[..., None])
    attn = jnp.einsum("bhnid,bhnjd->bhnij", q, k) * decay
    q_dec = q * jnp.exp(g_cum)[..., None]
    k_dec = k * jnp.exp(g_cum[..., -1:] - g_cum)[..., None]
    g_last = jnp.exp(g_cum[..., -1])

    to_scan = lambda t: jnp.moveaxis(t, 2, 0)
    xs = (to_scan(q_dec), to_scan(k_dec), to_scan(u), to_scan(w), to_scan(attn), to_scan(g_last))

    def step(state, inp):
        q_c, k_c, u_c, w_c, a_c, gl = inp
        v_new = u_c - jnp.einsum("bhid,bhde->bhie", w_c, state)
        o_c = jnp.einsum("bhid,bhde->bhie", q_c, state) + jnp.einsum("bhij,bhje->bhie", a_c, v_new)
        state = state * gl[..., None, None] + jnp.einsum("bhid,bhie->bhde", k_c, v_new)
        return state, o_c

    state0 = jnp.zeros((b_, h, dk, dv), dtype=q.dtype)
    _, o = lax.scan(step, state0, xs)
    return jnp.moveaxis(o, 0, 2).reshape(b_, h, s, dv)


def _sgu_branch(u, v, z, ln_g, ln_b, w_s, b_s):
    b_, s, _ = u.shape
    u = jax.nn.gelu(u, approximate=False)
    v = _layernorm(jax.nn.gelu(v, approximate=False), ln_g, ln_b)
    v = v.reshape(b_, s // SGU_CHUNK, SGU_CHUNK, SGU_GROUPS, SGU_GROUP_DIM)
    sp = jnp.einsum("gij,bcjgd->bcigd", w_s, v) + b_s.T[None, None, :, :, None]
    return u * sp.reshape(b_, s, D_SGU) * jax.nn.silu(z)


def _dn_gates(a, b, a_log, dt_bias):
    g = -jnp.exp(a_log.astype(jnp.float32)) * jax.nn.softplus(
        a.astype(jnp.float32) + dt_bias.astype(jnp.float32))
    beta = jax.nn.sigmoid(b.astype(jnp.float32))
    return g.transpose(0, 2, 1), beta.transpose(0, 2, 1)


def _deltanet_branch(qkv, z, a_f, a_b, b_f, b_b, conv_w, a_log_f, a_log_b,
                     dt_bias_f, dt_bias_b, norm_w):
    b_, s, _ = qkv.shape
    qkv = jax.nn.silu(_short_conv(qkv, conv_w))
    q, k, v = jnp.split(qkv, 3, axis=-1)
    heads = lambda t: t.reshape(b_, s, DN_HEADS, DN_HEAD_DIM).transpose(0, 2, 1, 3).astype(jnp.float32)
    q = _l2norm(heads(q)) * (DN_HEAD_DIM ** -0.5)
    k = _l2norm(heads(k))
    v = heads(v)
    g_f, beta_f = _dn_gates(a_f, b_f, a_log_f, dt_bias_f)
    g_b, beta_b = _dn_gates(a_b, b_b, a_log_b, dt_bias_b)
    rev = lambda t: jnp.flip(t, axis=2)
    o_f = _gated_delta_chunked(q, k, v, g_f, beta_f)
    o_b = rev(_gated_delta_chunked(rev(q), rev(k), rev(v), rev(g_b), rev(beta_b)))
    o = (o_f + o_b).transpose(0, 2, 1, 3)
    o = _rmsnorm(o, norm_w) * jax.nn.silu(z.reshape(b_, s, DN_HEADS, DN_HEAD_DIM).astype(jnp.float32))
    return o.reshape(b_, s, D_DN).astype(z.dtype)


def setup_inputs(seed: int = 0) -> dict:
    key = jax.random.key(seed)
    ks = jax.random.split(key, 16)
    f32 = jnp.float32
    x = jax.random.normal(ks[0], (BATCH, SEQ, D_MODEL), f32)
    norm_w = 1.0 + 0.02 * jax.random.normal(ks[1], (DEPTH, D_MODEL), f32)
    w_in = jax.random.normal(ks[2], (DEPTH, D_MODEL, D_IN), f32) * D_MODEL ** -0.5
    sgu_ln_g = 1.0 + 0.02 * jax.random.normal(ks[3], (DEPTH, D_SGU), f32)
    sgu_ln_b = 0.02 * jax.random.normal(ks[4], (DEPTH, D_SGU), f32)
    sgu_w = jax.random.normal(ks[5], (DEPTH, SGU_GROUPS, SGU_CHUNK, SGU_CHUNK), f32) * SGU_CHUNK ** -0.5
    sgu_b = 1.0 + 0.1 * jax.random.normal(ks[6], (DEPTH, SGU_GROUPS, SGU_CHUNK), f32)
    conv_w = jax.random.normal(ks[7], (DEPTH, CONV_WIDTH, 3 * D_DN), f32) * CONV_WIDTH ** -0.5
    a_log_f = jnp.log(jax.random.uniform(ks[8], (DEPTH, DN_HEADS), f32, 1.0, 16.0))
    a_log_b = jnp.log(jax.random.uniform(ks[9], (DEPTH, DN_HEADS), f32, 1.0, 16.0))
    dt_f = jnp.exp(jax.random.uniform(ks[10], (DEPTH, DN_HEADS), f32, np.log(1e-3), np.log(1e-1)))
    dt_b = jnp.exp(jax.random.uniform(ks[11], (DEPTH, DN_HEADS), f32, np.log(1e-3), np.log(1e-1)))
    dt_bias_f = dt_f + jnp.log(-jnp.expm1(-dt_f))
    dt_bias_b = dt_b + jnp.log(-jnp.expm1(-dt_b))
    dn_norm_w = 1.0 + 0.02 * jax.random.normal(ks[12], (DEPTH, DN_HEAD_DIM), f32)
    w_out = jax.random.normal(ks[13], (DEPTH, D_MIX, D_MODEL), f32) * D_MIX ** -0.5
    final_norm_w = 1.0 + 0.02 * jax.random.normal(ks[14], (D_MODEL,), f32)
    return {"x": x, "norm_w": norm_w, "w_in": w_in, "sgu_ln_g": sgu_ln_g,
            "sgu_ln_b": sgu_ln_b, "sgu_w": sgu_w, "sgu_b": sgu_b, "conv_w": conv_w,
            "a_log_f": a_log_f, "a_log_b": a_log_b, "dt_bias_f": dt_bias_f,
            "dt_bias_b": dt_bias_b, "dn_norm_w": dn_norm_w, "w_out": w_out,
            "final_norm_w": final_norm_w}


def reference(x, norm_w, w_in, sgu_ln_g, sgu_ln_b, sgu_w, sgu_b, conv_w,
              a_log_f, a_log_b, dt_bias_f, dt_bias_b, dn_norm_w, w_out,
              final_norm_w):
    for l in range(DEPTH):
        h = _rmsnorm(x, norm_w[l])
        proj = jnp.einsum("bsd,de->bse", h, w_in[l])
        u, v, z_a, qkv, z_b, a_f, a_b, b_f, b_b = _split_in(proj)
        y_a = _sgu_branch(u, v, z_a, sgu_ln_g[l], sgu_ln_b[l], sgu_w[l], sgu_b[l])
        y_b = _deltanet_branch(qkv, z_b, a_f, a_b, b_f, b_b, conv_w[l], a_log_f[l],
                               a_log_b[l], dt_bias_f[l], dt_bias_b[l], dn_norm_w[l])
        y = jnp.concatenate([y_a, y_b], axis=-1)
        x = x + jnp.einsum("bse,ed->bsd", y, w_out[l])
    return _rmsnorm(x, final_norm_w)
```

```python
import functools

import jax
import jax.numpy as jnp
from jax import lax
from jax.experimental import pallas as pl
from jax.experimental.pallas import tpu as pltpu

D_MODEL = 2048
D_SGU = 1024
SGU_GROUPS = 8
SGU_GROUP_DIM = 128
SGU_CHUNK = 128
D_DN = 1024
DN_HEADS = 8
DN_HEAD_DIM = 128
DN_CHUNK = 64
CONV_WIDTH = 5
CONV_PAD = CONV_WIDTH // 2
NORM_EPS = 1e-6
D_MAIN = 3 * D_SGU + 4 * D_DN
N_GATE = 4 * DN_HEADS
LANES = 128
SUBLANES = 8
COL_BLOCK = 1024

CB_U, CB_V, CB_ZA, CB_Q, CB_ZB = 0, 1, 2, 3, 6
GC_FWD, GC_BWD, BETA_FWD, BETA_BWD = 0, DN_HEADS, 2 * DN_HEADS, 3 * DN_HEADS

F32 = jnp.float32
BF16 = jnp.bfloat16
HIGHEST = lax.Precision.HIGHEST
TINV_PRECISION = HIGHEST

TM_IN = 512
TN_IN = 1024
R_GATE = 256
R_SGU = 256
R_PREP = 256
TM_OUT = 256
VMEM_LIMIT = 48 * 1024 * 1024


def _sigmoid(x):
    return 1.0 / (1.0 + jnp.exp(-x))


def _gelu(x):
    return 0.5 * x * (1.0 + lax.erf(x * (2.0 ** -0.5)))


def _nt_dot(a, b, precision=None):
    return lax.dot_general(a, b, (((1,), (1,)), ((), ())), precision=precision,
                           preferred_element_type=F32)


def _tn_dot(a, b):
    return lax.dot_general(a, b, (((0,), (0,)), ((), ())), preferred_element_type=F32)


def _inproj_kernel(x_ref, nw_ref, w_ref, wg_ref, out_ref, gate_ref, hb_ref):
    @pl.when(pl.program_id(1) == 0)
    def _():
        x = x_ref[...]
        ms = jnp.mean(x * x, axis=-1, keepdims=True)
        h = x * lax.rsqrt(ms + NORM_EPS) * nw_ref[...]
        hb_ref[...] = h.astype(BF16)
        gate_ref[...] = jnp.dot(h, wg_ref[...], precision=HIGHEST,
                                preferred_element_type=F32)

    out_ref[...] = jnp.dot(hb_ref[...], w_ref[...], preferred_element_type=F32)


def _inproj(x2, norm_w, w_main, w_gate):
    t = x2.shape[0]
    return pl.pallas_call(
        _inproj_kernel,
        grid=(t // TM_IN, D_MAIN // TN_IN),
        in_specs=[
            pl.BlockSpec((TM_IN, D_MODEL), lambda i, j: (i, 0)),
            pl.BlockSpec((1, D_MODEL), lambda i, j: (0, 0)),
            pl.BlockSpec((D_MODEL, TN_IN), lambda i, j: (0, j)),
            pl.BlockSpec((D_MODEL, LANES), lambda i, j: (0, 0)),
        ],
        out_specs=[
            pl.BlockSpec((TM_IN, TN_IN), lambda i, j: (i, j)),
            pl.BlockSpec((TM_IN, LANES), lambda i, j: (i, 0)),
        ],
        out_shape=[
            jax.ShapeDtypeStruct((t, D_MAIN), F32),
            jax.ShapeDtypeStruct((t, LANES), F32),
        ],
        scratch_shapes=[pltpu.VMEM((TM_IN, D_MODEL), BF16)],
        compiler_params=pltpu.CompilerParams(
            dimension_semantics=("parallel", "arbitrary"),
            vmem_limit_bytes=VMEM_LIMIT),
        name="inproj",
    )(x2, norm_w, w_main, w_gate)


def _gate_kernel(g_ref, alog_ref, dtb_ref, gc_ref, gt_ref):
    r = g_ref.shape[0]
    raw = g_ref[...]
    lane = lax.broadcasted_iota(jnp.int32, raw.shape, 1)
    sp_in = raw + dtb_ref[...]
    softplus = jnp.maximum(sp_in, 0.0) + jnp.log1p(jnp.exp(-jnp.abs(sp_in)))
    g = jnp.where(lane < BETA_FWD, -jnp.exp(alog_ref[...]) * softplus, 0.0)
    beta = _sigmoid(raw)
    ri = lax.broadcasted_iota(jnp.int32, (r, r), 0)
    ci = lax.broadcasted_iota(jnp.int32, (r, r), 1)
    same = (ri // DN_CHUNK) == (ci // DN_CHUNK)
    lower = jnp.where(same & (ri >= ci), 1.0, 0.0).astype(F32)
    upper = jnp.where(same & (ri <= ci), 1.0, 0.0).astype(F32)
    pre = jnp.dot(lower, g, precision=HIGHEST, preferred_element_type=F32)
    suf = jnp.dot(upper, g, precision=HIGHEST, preferred_element_type=F32)
    out = jnp.where(lane < GC_BWD, pre,
                    jnp.where(lane < BETA_FWD, suf,
                              jnp.where(lane < N_GATE, beta, 0.0)))
    gc_ref[...] = out
    out_t = out.T
    for c in range(r // DN_CHUNK):
        gt_ref[c] = out_t[0:N_GATE, c * DN_CHUNK:(c + 1) * DN_CHUNK]


def _gates(gates_raw, alog_row, dtb_row):
    t = gates_raw.shape[0]
    return pl.pallas_call(
        _gate_kernel,
        grid=(t // R_GATE,),
        in_specs=[
            pl.BlockSpec((R_GATE, LANES), lambda i: (i, 0)),
            pl.BlockSpec((1, LANES), lambda i: (0, 0)),
            pl.BlockSpec((1, LANES), lambda i: (0, 0)),
        ],
        out_specs=[
            pl.BlockSpec((R_GATE, LANES), lambda i: (i, 0)),
            pl.BlockSpec((R_GATE // DN_CHUNK, N_GATE, DN_CHUNK), lambda i: (i, 0, 0)),
        ],
        out_shape=[
            jax.ShapeDtypeStruct((t, LANES), F32),
            jax.ShapeDtypeStruct((t // DN_CHUNK, N_GATE, DN_CHUNK), F32),
        ],
        compiler_params=pltpu.CompilerParams(dimension_semantics=("parallel",)),
        name="gates",
    )(gates_raw, alog_row, dtb_row)


def _sgu_kernel(u_ref, v_ref, z_ref, lg_ref, lb_ref, ws_ref, bs_ref, o_ref, vn_ref):
    r = u_ref.shape[0]
    v = _gelu(v_ref[...])
    mu = jnp.mean(v, axis=-1, keepdims=True)
    vc = v - mu
    var = jnp.mean(vc * vc, axis=-1, keepdims=True)
    vn = vc * lax.rsqrt(var + NORM_EPS) * lg_ref[...] + lb_ref[...]
    vn_ref[...] = vn.astype(BF16)
    for c in range(r // SGU_CHUNK):
        rows = slice(c * SGU_CHUNK, (c + 1) * SGU_CHUNK)
        for g in range(SGU_GROUPS):
            cols = slice(g * SGU_GROUP_DIM, (g + 1) * SGU_GROUP_DIM)
            sp = jnp.dot(ws_ref[g], vn_ref[rows, cols], preferred_element_type=F32)
            sp = sp + bs_ref[:, g:g + 1]
            u = _gelu(u_ref[rows, cols])
            z = z_ref[rows, cols]
            o_ref[rows, cols] = u * sp * (z * _sigmoid(z))


def _sgu(proj, ln_g, ln_b, ws_bf16, bs_t):
    t = proj.shape[0]
    return pl.pallas_call(
        _sgu_kernel,
        grid=(t // R_SGU,),
        in_specs=[
            pl.BlockSpec((R_SGU, COL_BLOCK), lambda i: (i, CB_U)),
            pl.BlockSpec((R_SGU, COL_BLOCK), lambda i: (i, CB_V)),
            pl.BlockSpec((R_SGU, COL_BLOCK), lambda i: (i, CB_ZA)),
            pl.BlockSpec((1, D_SGU), lambda i: (0, 0)),
            pl.BlockSpec((1, D_SGU), lambda i: (0, 0)),
            pl.BlockSpec((SGU_GROUPS, SGU_CHUNK, SGU_CHUNK), lambda i: (0, 0, 0)),
            pl.BlockSpec((SGU_CHUNK, SGU_GROUPS), lambda i: (0, 0)),
        ],
        out_specs=pl.BlockSpec((R_SGU, D_SGU), lambda i: (i, 0)),
        out_shape=jax.ShapeDtypeStruct((t, D_SGU), F32),
        scratch_shapes=[pltpu.VMEM((R_SGU, D_SGU), BF16)],
        compiler_params=pltpu.CompilerParams(
            dimension_semantics=("parallel",), vmem_limit_bytes=VMEM_LIMIT),
        name="sgu",
    )(proj, proj, proj, ln_g, ln_b, ws_bf16, bs_t)


def _prep_kernel(x_ref, p_ref, n_ref, cw_ref, o_ref, pad_ref):
    kind = pl.program_id(0)
    i = pl.program_id(2)
    r = x_ref.shape[1]
    pad_ref[0:SUBLANES, :] = jnp.where(i > 0, p_ref[0], 0.0)
    pad_ref[SUBLANES:SUBLANES + r, :] = x_ref[0]
    pad_ref[SUBLANES + r:2 * SUBLANES + r, :] = jnp.where(
        i < pl.num_programs(2) - 1, n_ref[0], 0.0)
    base = SUBLANES - CONV_PAD
    acc = cw_ref[0:1, :] * pad_ref[base:base + r, :]
    for j in range(1, CONV_WIDTH):
        acc = acc + cw_ref[j:j + 1, :] * pad_ref[base + j:base + j + r, :]
    y = acc * _sigmoid(acc)

    @pl.when(kind == 2)
    def _():
        o_ref[0, 0] = y

    @pl.when(kind < 2)
    def _():
        scale = jnp.where(kind == 0, DN_HEAD_DIM ** -0.5, 1.0).astype(F32)
        for h in range(DN_HEADS):
            cols = slice(h * DN_HEAD_DIM, (h + 1) * DN_HEAD_DIM)
            yh = y[:, cols]
            ss = jnp.sum(yh * yh, axis=-1, keepdims=True)
            o_ref[0, 0, :, cols] = yh * lax.rsqrt(ss + NORM_EPS) * scale


def _prep(proj3, conv_w):
    b, s, _ = proj3.shape
    nblk8 = s // SUBLANES
    rb = R_PREP // SUBLANES
    return pl.pallas_call(
        _prep_kernel,
        grid=(3, b, s // R_PREP),
        in_specs=[
            pl.BlockSpec((1, R_PREP, COL_BLOCK), lambda kd, bb, i: (bb, i, CB_Q + kd)),
            pl.BlockSpec((1, SUBLANES, COL_BLOCK),
                         lambda kd, bb, i: (bb, jnp.maximum(i * rb - 1, 0), CB_Q + kd)),
            pl.BlockSpec((1, SUBLANES, COL_BLOCK),
                         lambda kd, bb, i: (bb, jnp.minimum((i + 1) * rb, nblk8 - 1), CB_Q + kd)),
            pl.BlockSpec((CONV_WIDTH, COL_BLOCK), lambda kd, bb, i: (0, kd)),
        ],
        out_specs=pl.BlockSpec((1, 1, R_PREP, D_DN), lambda kd, bb, i: (kd, bb, i, 0)),
        out_shape=jax.ShapeDtypeStruct((3, b, s, D_DN), F32),
        scratch_shapes=[pltpu.VMEM((R_PREP + 2 * SUBLANES, COL_BLOCK), F32)],
        compiler_params=pltpu.CompilerParams(
            dimension_semantics=("parallel", "parallel", "parallel"),
            vmem_limit_bytes=VMEM_LIMIT),
        name="prep",
    )(proj3, proj3, proj3, conv_w)


def _tri_inverse(m, eye, ri, ci):
    dot = functools.partial(jnp.dot, precision=TINV_PRECISION, preferred_element_type=F32)
    blk = SUBLANES
    same = (ri // blk) == (ci // blk)
    p = jnp.where(same, m, 0.0)
    d = eye - p
    p = dot(p, p)
    d = d + dot(d, p)
    p = dot(p, p)
    d = d + dot(d, p)
    while blk < DN_CHUNK:
        same2 = (ri // (2 * blk)) == (ci // (2 * blk))
        off = jnp.where(same2 & jnp.logical_not(same), m, 0.0)
        d = d - dot(dot(d, off), d)
        same = same2
        blk *= 2
    return d


def _delta_kernel(q_ref, k_ref, v_ref, gc_ref, gt_ref, o_ref, s_ref, *, reverse):
    @pl.when(pl.program_id(1) == 0)
    def _():
        s_ref[...] = jnp.zeros_like(s_ref)

    c = DN_CHUNK
    ri = lax.broadcasted_iota(jnp.int32, (c, c), 0)
    ci = lax.broadcasted_iota(jnp.int32, (c, c), 1)
    if reverse:
        incl, strict = ri <= ci, ri < ci
        g_off, b_off, last = GC_BWD, BETA_BWD, 0
    else:
        incl, strict = ri >= ci, ri > ci
        g_off, b_off, last = GC_FWD, BETA_FWD, c - 1
    eye = jnp.where(ri == ci, 1.0, 0.0).astype(F32)
    gcols = gc_ref[0]
    grows = gt_ref[0, 0]
    for h in range(DN_HEADS):
        cols = slice(h * DN_HEAD_DIM, (h + 1) * DN_HEAD_DIM)
        q = q_ref[0, 0, :, cols]
        k = k_ref[0, 0, :, cols]
        v = v_ref[0, 0, :, cols]
        gcol = gcols[:, g_off + h:g_off + h + 1]
        bcol = gcols[:, b_off + h:b_off + h + 1]
        grow = grows[g_off + h:g_off + h + 1, :]
        glast = grow[:, last:last + 1]
        decay = jnp.where(incl, jnp.exp(jnp.where(incl, gcol - grow, 0.0)), 0.0)
        kb = k.astype(BF16)
        kk = _nt_dot(kb, kb)
        qk = _nt_dot(q.astype(BF16), kb)
        m = jnp.where(strict, kk * bcol * decay, 0.0)
        t_inv = _tri_inverse(m, eye, ri, ci).astype(BF16)
        eg = jnp.exp(gcol)
        u = jnp.dot(t_inv, (v * bcol).astype(BF16), preferred_element_type=F32)
        w = jnp.dot(t_inv, (k * bcol * eg).astype(BF16), preferred_element_type=F32)
        attn = qk * decay
        q_dec = q * eg
        k_dec = k * jnp.exp(glast - gcol)
        state = s_ref[h]
        state_b = state.astype(BF16)
        v_new = u - jnp.dot(w.astype(BF16), state_b, preferred_element_type=F32)
        v_new_b = v_new.astype(BF16)
        o = (jnp.dot(q_dec.astype(BF16), state_b, preferred_element_type=F32)
             + jnp.dot(attn.astype(BF16), v_new_b, preferred_element_type=F32))
        s_ref[h] = state * jnp.exp(glast) + _tn_dot(k_dec.astype(BF16), v_new_b)
        o_ref[0, :, cols] = o


def _delta(qkvn, gc3, gt4, *, reverse):
    _, b, s, _ = qkvn.shape
    nc = s // DN_CHUNK
    if reverse:
        cidx = lambda i: nc - 1 - i
    else:
        cidx = lambda i: i
    return pl.pallas_call(
        functools.partial(_delta_kernel, reverse=reverse),
        grid=(b, nc),
        in_specs=[
            pl.BlockSpec((1, 1, DN_CHUNK, D_DN), lambda bb, i: (0, bb, cidx(i), 0)),
            pl.BlockSpec((1, 1, DN_CHUNK, D_DN), lambda bb, i: (1, bb, cidx(i), 0)),
            pl.BlockSpec((1, 1, DN_CHUNK, D_DN), lambda bb, i: (2, bb, cidx(i), 0)),
            pl.BlockSpec((1, DN_CHUNK, LANES), lambda bb, i: (bb, cidx(i), 0)),
            pl.BlockSpec((1, 1, N_GATE, DN_CHUNK), lambda bb, i: (bb, cidx(i), 0, 0)),
        ],
        out_specs=pl.BlockSpec((1, DN_CHUNK, D_DN), lambda bb, i: (bb, cidx(i), 0)),
        out_shape=jax.ShapeDtypeStruct((b, s, D_DN), F32),
        scratch_shapes=[pltpu.VMEM((DN_HEADS, DN_HEAD_DIM, DN_HEAD_DIM), F32)],
        compiler_params=pltpu.CompilerParams(
            dimension_semantics=("parallel", "arbitrary"),
            vmem_limit_bytes=VMEM_LIMIT),
        name="delta_bwd" if reverse else "delta_fwd",
    )(qkvn, qkvn, qkvn, gc3, gt4)


def _outproj_kernel(x_ref, ya_ref, of_ref, ob_ref, zb_ref, nw_ref, w_ref, fw_ref,
                    o_ref, y_ref, *, final):
    y_ref[:, 0:D_SGU] = ya_ref[...].astype(BF16)
    for h in range(DN_HEADS):
        cols = slice(h * DN_HEAD_DIM, (h + 1) * DN_HEAD_DIM)
        o = of_ref[:, cols] + ob_ref[:, cols]
        ms = jnp.mean(o * o, axis=-1, keepdims=True)
        z = zb_ref[:, cols]
        yh = o * lax.rsqrt(ms + NORM_EPS) * nw_ref[...] * (z * _sigmoid(z))
        y_ref[:, D_SGU + h * DN_HEAD_DIM:D_SGU + (h + 1) * DN_HEAD_DIM] = yh.astype(BF16)
    xn = x_ref[...] + jnp.dot(y_ref[...], w_ref[...], preferred_element_type=F32)
    if final:
        ms = jnp.mean(xn * xn, axis=-1, keepdims=True)
        xn = xn * lax.rsqrt(ms + NORM_EPS) * fw_ref[...]
    o_ref[...] = xn


def _outproj(x2, y_a, o_f, o_b, proj, dn_norm_w, w_out_bf16, final_w, *, final):
    t = x2.shape[0]
    return pl.pallas_call(
        functools.partial(_outproj_kernel, final=final),
        grid=(t // TM_OUT,),
        in_specs=[
            pl.BlockSpec((TM_OUT, D_MODEL), lambda i: (i, 0)),
            pl.BlockSpec((TM_OUT, D_SGU), lambda i: (i, 0)),
            pl.BlockSpec((TM_OUT, D_DN), lambda i: (i, 0)),
            pl.BlockSpec((TM_OUT, D_DN), lambda i: (i, 0)),
            pl.BlockSpec((TM_OUT, COL_BLOCK), lambda i: (i, CB_ZB)),
            pl.BlockSpec((1, DN_HEAD_DIM), lambda i: (0, 0)),
            pl.BlockSpec((D_MODEL, D_MODEL), lambda i: (0, 0)),
            pl.BlockSpec((1, D_MODEL), lambda i: (0, 0)),
        ],
        out_specs=pl.BlockSpec((TM_OUT, D_MODEL), lambda i: (i, 0)),
        out_shape=jax.ShapeDtypeStruct((t, D_MODEL), F32),
        scratch_shapes=[pltpu.VMEM((TM_OUT, D_MODEL), BF16)],
        compiler_params=pltpu.CompilerParams(
            dimension_semantics=("parallel",), vmem_limit_bytes=VMEM_LIMIT),
        name="outproj_final" if final else "outproj",
    )(x2, y_a, o_f, o_b, proj, dn_norm_w, w_out_bf16, final_w)


def _pad_lanes(row):
    return jnp.pad(row, (0, LANES - row.shape[0]))[None, :]


def kernel(x, norm_w, w_in, sgu_ln_g, sgu_ln_b, sgu_w, sgu_b, conv_w, a_log_f, a_log_b,
           dt_bias_f, dt_bias_b, dn_norm_w, w_out, final_norm_w):
    b, s, _ = x.shape
    t = b * s
    depth = w_in.shape[0]
    x2 = x.reshape(t, D_MODEL)
    for l in range(depth):
        w_main = w_in[l, :, :D_MAIN].astype(BF16)
        w_gate = jnp.pad(w_in[l, :, D_MAIN:], ((0, 0), (0, LANES - N_GATE)))
        proj, gates_raw = _inproj(x2, norm_w[l][None, :], w_main, w_gate)
        alog_row = _pad_lanes(jnp.concatenate([a_log_f[l], a_log_b[l]]))
        dtb_row = _pad_lanes(jnp.concatenate([dt_bias_f[l], dt_bias_b[l]]))
        gc, gt = _gates(gates_raw, alog_row, dtb_row)
        y_a = _sgu(proj, sgu_ln_g[l][None, :], sgu_ln_b[l][None, :],
                   sgu_w[l].astype(BF16), sgu_b[l].T)
        qkvn = _prep(proj.reshape(b, s, D_MAIN), conv_w[l])
        gc3 = gc.reshape(b, s, LANES)
        gt4 = gt.reshape(b, s // DN_CHUNK, N_GATE, DN_CHUNK)
        o_f = _delta(qkvn, gc3, gt4, reverse=False).reshape(t, D_DN)
        o_b = _delta(qkvn, gc3, gt4, reverse=True).reshape(t, D_DN)
        x2 = _outproj(x2, y_a, o_f, o_b, proj, dn_norm_w[l][None, :],
                      w_out[l].astype(BF16), final_norm_w[None, :],
                      final=(l == depth - 1))
    return x2.reshape(b, s, D_MODEL)
```

```python
import functools

import jax
import jax.numpy as jnp
from jax import lax
from jax.experimental import pallas as pl
from jax.experimental.pallas import tpu as pltpu

D_MODEL = 2048
D_SGU = 1024
SGU_GROUPS = 8
SGU_GROUP_DIM = 128
SGU_CHUNK = 128
D_DN = 1024
DN_HEADS = 8
DN_HEAD_DIM = 128
DN_CHUNK = 128
CONV_WIDTH = 5
CONV_PAD = CONV_WIDTH // 2
NORM_EPS = 1e-6
D_MAIN = 3 * D_SGU + 4 * D_DN
N_GATE = 4 * DN_HEADS
LANES = 128
SUBLANES = 8
COL_BLOCK = 1024

CB_U, CB_V, CB_ZA, CB_Q, CB_ZB = 0, 1, 2, 3, 6
GC_FWD, GC_BWD, BETA_FWD, BETA_BWD = 0, DN_HEADS, 2 * DN_HEADS, 3 * DN_HEADS

F32 = jnp.float32
BF16 = jnp.bfloat16
HIGHEST = lax.Precision.HIGHEST

TM_IN = 512
TN_IN = 1024
R_GATE = 256
R_SGU = 256
R_PREP = 256
TM_OUT = 256
VMEM_LIMIT = 48 * 1024 * 1024


def _sigmoid(x):
    return 1.0 / (1.0 + jnp.exp(-x))


def _gelu(x):
    return 0.5 * x * (1.0 + lax.erf(x * (2.0 ** -0.5)))


def _nt_dot(a, b, precision=None):
    return lax.dot_general(a, b, (((1,), (1,)), ((), ())), precision=precision,
                           preferred_element_type=F32)


def _tn_dot(a, b):
    return lax.dot_general(a, b, (((0,), (0,)), ((), ())), preferred_element_type=F32)


def _inproj_kernel(x_ref, nw_ref, w_ref, wg_ref, out_ref, gate_ref, hb_ref):
    @pl.when(pl.program_id(1) == 0)
    def _():
        x = x_ref[...]
        ms = jnp.mean(x * x, axis=-1, keepdims=True)
        h = x * lax.rsqrt(ms + NORM_EPS) * nw_ref[...]
        hb_ref[...] = h.astype(BF16)
        gate_ref[...] = jnp.dot(h, wg_ref[...], precision=HIGHEST,
                                preferred_element_type=F32)

    out_ref[...] = jnp.dot(hb_ref[...], w_ref[...], preferred_element_type=F32)


def _inproj(x2, norm_w, w_main, w_gate):
    t = x2.shape[0]
    return pl.pallas_call(
        _inproj_kernel,
        grid=(t // TM_IN, D_MAIN // TN_IN),
        in_specs=[
            pl.BlockSpec((TM_IN, D_MODEL), lambda i, j: (i, 0)),
            pl.BlockSpec((1, D_MODEL), lambda i, j: (0, 0)),
            pl.BlockSpec((D_MODEL, TN_IN), lambda i, j: (0, j)),
            pl.BlockSpec((D_MODEL, LANES), lambda i, j: (0, 0)),
        ],
        out_specs=[
            pl.BlockSpec((TM_IN, TN_IN), lambda i, j: (i, j)),
            pl.BlockSpec((TM_IN, LANES), lambda i, j: (i, 0)),
        ],
        out_shape=[
            jax.ShapeDtypeStruct((t, D_MAIN), F32),
            jax.ShapeDtypeStruct((t, LANES), F32),
        ],
        scratch_shapes=[pltpu.VMEM((TM_IN, D_MODEL), BF16)],
        compiler_params=pltpu.CompilerParams(
            dimension_semantics=("parallel", "arbitrary"),
            vmem_limit_bytes=VMEM_LIMIT),
        name="inproj",
    )(x2, norm_w, w_main, w_gate)


def _gate_kernel(g_ref, alog_ref, dtb_ref, gc_ref, gt_ref):
    r = g_ref.shape[0]
    raw = g_ref[...]
    lane = lax.broadcasted_iota(jnp.int32, raw.shape, 1)
    sp_in = raw + dtb_ref[...]
    softplus = jnp.maximum(sp_in, 0.0) + jnp.log1p(jnp.exp(-jnp.abs(sp_in)))
    g = jnp.where(lane < BETA_FWD, -jnp.exp(alog_ref[...]) * softplus, 0.0)
    beta = _sigmoid(raw)
    ri = lax.broadcasted_iota(jnp.int32, (r, r), 0)
    ci = lax.broadcasted_iota(jnp.int32, (r, r), 1)
    same = (ri // DN_CHUNK) == (ci // DN_CHUNK)
    lower = jnp.where(same & (ri >= ci), 1.0, 0.0).astype(F32)
    upper = jnp.where(same & (ri <= ci), 1.0, 0.0).astype(F32)
    pre = jnp.dot(lower, g, precision=HIGHEST, preferred_element_type=F32)
    suf = jnp.dot(upper, g, precision=HIGHEST, preferred_element_type=F32)
    out = jnp.where(lane < GC_BWD, pre,
                    jnp.where(lane < BETA_FWD, suf,
                              jnp.where(lane < N_GATE, beta, 0.0)))
    gc_ref[...] = out
    out_t = out.T
    for c in range(r // DN_CHUNK):
        gt_ref[c] = out_t[0:N_GATE, c * DN_CHUNK:(c + 1) * DN_CHUNK]


def _gates(gates_raw, alog_row, dtb_row):
    t = gates_raw.shape[0]
    return pl.pallas_call(
        _gate_kernel,
        grid=(t // R_GATE,),
        in_specs=[
            pl.BlockSpec((R_GATE, LANES), lambda i: (i, 0)),
            pl.BlockSpec((1, LANES), lambda i: (0, 0)),
            pl.BlockSpec((1, LANES), lambda i: (0, 0)),
        ],
        out_specs=[
            pl.BlockSpec((R_GATE, LANES), lambda i: (i, 0)),
            pl.BlockSpec((R_GATE // DN_CHUNK, N_GATE, DN_CHUNK), lambda i: (i, 0, 0)),
        ],
        out_shape=[
            jax.ShapeDtypeStruct((t, LANES), F32),
            jax.ShapeDtypeStruct((t // DN_CHUNK, N_GATE, DN_CHUNK), F32),
        ],
        compiler_params=pltpu.CompilerParams(dimension_semantics=("parallel",)),
        name="gates",
    )(gates_raw, alog_row, dtb_row)


def _sgu_kernel(u_ref, v_ref, z_ref, lg_ref, lb_ref, ws_ref, bs_ref, o_ref, vn_ref):
    r = u_ref.shape[0]
    v = _gelu(v_ref[...])
    mu = jnp.mean(v, axis=-1, keepdims=True)
    vc = v - mu
    var = jnp.mean(vc * vc, axis=-1, keepdims=True)
    vn = vc * lax.rsqrt(var + NORM_EPS) * lg_ref[...] + lb_ref[...]
    vn_ref[...] = vn.astype(BF16)
    for c in range(r // SGU_CHUNK):
        rows = slice(c * SGU_CHUNK, (c + 1) * SGU_CHUNK)
        for g in range(SGU_GROUPS):
            cols = slice(g * SGU_GROUP_DIM, (g + 1) * SGU_GROUP_DIM)
            sp = jnp.dot(ws_ref[g], vn_ref[rows, cols], preferred_element_type=F32)
            sp = sp + bs_ref[:, g:g + 1]
            u = _gelu(u_ref[rows, cols])
            z = z_ref[rows, cols]
            o_ref[rows, cols] = u * sp * (z * _sigmoid(z))


def _sgu(proj, ln_g, ln_b, ws_bf16, bs_t):
    t = proj.shape[0]
    return pl.pallas_call(
        _sgu_kernel,
        grid=(t // R_SGU,),
        in_specs=[
            pl.BlockSpec((R_SGU, COL_BLOCK), lambda i: (i, CB_U)),
            pl.BlockSpec((R_SGU, COL_BLOCK), lambda i: (i, CB_V)),
            pl.BlockSpec((R_SGU, COL_BLOCK), lambda i: (i, CB_ZA)),
            pl.BlockSpec((1, D_SGU), lambda i: (0, 0)),
            pl.BlockSpec((1, D_SGU), lambda i: (0, 0)),
            pl.BlockSpec((SGU_GROUPS, SGU_CHUNK, SGU_CHUNK), lambda i: (0, 0, 0)),
            pl.BlockSpec((SGU_CHUNK, SGU_GROUPS), lambda i: (0, 0)),
        ],
        out_specs=pl.BlockSpec((R_SGU, D_SGU), lambda i: (i, 0)),
        out_shape=jax.ShapeDtypeStruct((t, D_SGU), F32),
        scratch_shapes=[pltpu.VMEM((R_SGU, D_SGU), BF16)],
        compiler_params=pltpu.CompilerParams(
            dimension_semantics=("parallel",), vmem_limit_bytes=VMEM_LIMIT),
        name="sgu",
    )(proj, proj, proj, ln_g, ln_b, ws_bf16, bs_t)


def _prep_kernel(x_ref, p_ref, n_ref, cw_ref, o_ref, pad_ref):
    kind = pl.program_id(0)
    i = pl.program_id(2)
    r = x_ref.shape[1]
    pad_ref[0:SUBLANES, :] = jnp.where(i > 0, p_ref[0], 0.0)
    pad_ref[SUBLANES:SUBLANES + r, :] = x_ref[0]
    pad_ref[SUBLANES + r:2 * SUBLANES + r, :] = jnp.where(
        i < pl.num_programs(2) - 1, n_ref[0], 0.0)
    base = SUBLANES - CONV_PAD
    acc = cw_ref[0:1, :] * pad_ref[base:base + r, :]
    for j in range(1, CONV_WIDTH):
        acc = acc + cw_ref[j:j + 1, :] * pad_ref[base + j:base + j + r, :]
    y = acc * _sigmoid(acc)

    @pl.when(kind == 2)
    def _():
        o_ref[0, 0] = y

    @pl.when(kind < 2)
    def _():
        scale = jnp.where(kind == 0, DN_HEAD_DIM ** -0.5, 1.0).astype(F32)
        for h in range(DN_HEADS):
            cols = slice(h * DN_HEAD_DIM, (h + 1) * DN_HEAD_DIM)
            yh = y[:, cols]
            ss = jnp.sum(yh * yh, axis=-1, keepdims=True)
            o_ref[0, 0, :, cols] = yh * lax.rsqrt(ss + NORM_EPS) * scale


def _prep(proj3, conv_w):
    b, s, _ = proj3.shape
    nblk8 = s // SUBLANES
    rb = R_PREP // SUBLANES
    return pl.pallas_call(
        _prep_kernel,
        grid=(3, b, s // R_PREP),
        in_specs=[
            pl.BlockSpec((1, R_PREP, COL_BLOCK), lambda kd, bb, i: (bb, i, CB_Q + kd)),
            pl.BlockSpec((1, SUBLANES, COL_BLOCK),
                         lambda kd, bb, i: (bb, jnp.maximum(i * rb - 1, 0), CB_Q + kd)),
            pl.BlockSpec((1, SUBLANES, COL_BLOCK),
                         lambda kd, bb, i: (bb, jnp.minimum((i + 1) * rb, nblk8 - 1), CB_Q + kd)),
            pl.BlockSpec((CONV_WIDTH, COL_BLOCK), lambda kd, bb, i: (0, kd)),
        ],
        out_specs=pl.BlockSpec((1, 1, R_PREP, D_DN), lambda kd, bb, i: (kd, bb, i, 0)),
        out_shape=jax.ShapeDtypeStruct((3, b, s, D_DN), F32),
        scratch_shapes=[pltpu.VMEM((R_PREP + 2 * SUBLANES, COL_BLOCK), F32)],
        compiler_params=pltpu.CompilerParams(
            dimension_semantics=("parallel", "parallel", "parallel"),
            vmem_limit_bytes=VMEM_LIMIT),
        name="prep",
    )(proj3, proj3, proj3, conv_w)


def _mm(a, b):
    return jnp.dot(a.astype(BF16), b.astype(BF16), preferred_element_type=F32)


def _tri_inverse(ms, eye, ri, ci):
    blk = SUBLANES
    same = (ri // blk) == (ci // blk)
    p = [jnp.where(same, m, 0.0).astype(BF16) for m in ms]
    d = [eye - x for x in p]
    p = [_mm(x, x).astype(BF16) for x in p]
    d = [x + _mm(x, y) for x, y in zip(d, p)]
    p = [_mm(x, x).astype(BF16) for x in p]
    d = [x + _mm(x, y) for x, y in zip(d, p)]
    while blk < DN_CHUNK:
        same2 = (ri // (2 * blk)) == (ci // (2 * blk))
        sel = same2 & jnp.logical_not(same)
        off = [jnp.where(sel, m, 0.0).astype(BF16) for m in ms]
        db = [x.astype(BF16) for x in d]
        t = [_mm(x, y) for x, y in zip(db, off)]
        d = [x - _mm(y, z) for x, y, z in zip(d, t, db)]
        same = same2
        blk *= 2
    return d


def _delta_kernel(q_ref, k_ref, v_ref, gc_ref, gt_ref, o_ref, s_ref, *, reverse):
    @pl.when(pl.program_id(1) == 0)
    def _():
        s_ref[...] = jnp.zeros_like(s_ref)

    c = DN_CHUNK
    heads = range(DN_HEADS)
    ri = lax.broadcasted_iota(jnp.int32, (c, c), 0)
    ci = lax.broadcasted_iota(jnp.int32, (c, c), 1)
    if reverse:
        incl, strict = ri <= ci, ri < ci
        g_off, b_off, last = GC_BWD, BETA_BWD, 0
    else:
        incl, strict = ri >= ci, ri > ci
        g_off, b_off, last = GC_FWD, BETA_FWD, c - 1
    eye = jnp.where(ri == ci, 1.0, 0.0).astype(F32)
    gcols = gc_ref[0]
    grows = gt_ref[0, 0]
    cols = [slice(h * DN_HEAD_DIM, (h + 1) * DN_HEAD_DIM) for h in heads]
    q = [q_ref[0, 0, :, cols[h]] for h in heads]
    k = [k_ref[0, 0, :, cols[h]] for h in heads]
    v = [v_ref[0, 0, :, cols[h]] for h in heads]
    gcol = [gcols[:, g_off + h:g_off + h + 1] for h in heads]
    bcol = [gcols[:, b_off + h:b_off + h + 1] for h in heads]
    grow = [grows[g_off + h:g_off + h + 1, :] for h in heads]
    glast = [grow[h][:, last:last + 1] for h in heads]
    kb = [k[h].astype(BF16) for h in heads]
    kk = [_nt_dot(kb[h], kb[h]) for h in heads]
    qk = [_nt_dot(q[h].astype(BF16), kb[h]) for h in heads]
    decay = [jnp.where(incl, jnp.exp(jnp.where(incl, gcol[h] - grow[h], 0.0)), 0.0)
             for h in heads]
    m = [jnp.where(strict, kk[h] * bcol[h] * decay[h], 0.0) for h in heads]
    t_inv = _tri_inverse(m, eye, ri, ci)
    eg = [jnp.exp(gcol[h]) for h in heads]
    rhs = [jnp.concatenate([v[h] * bcol[h], k[h] * bcol[h] * eg[h]], axis=1) for h in heads]
    uw = [_mm(t_inv[h], rhs[h]) for h in heads]
    attn = [(qk[h] * decay[h]).astype(BF16) for h in heads]
    lhs = [jnp.concatenate([uw[h][:, DN_HEAD_DIM:], q[h] * eg[h]], axis=0) for h in heads]
    state = [s_ref[h] for h in heads]
    ws = [_mm(lhs[h], state[h]) for h in heads]
    v_new = [(uw[h][:, :DN_HEAD_DIM] - ws[h][:c]).astype(BF16) for h in heads]
    k_dec = [(k[h] * jnp.exp(glast[h] - gcol[h])).astype(BF16) for h in heads]
    o = [ws[h][c:] + jnp.dot(attn[h], v_new[h], preferred_element_type=F32) for h in heads]
    s_new = [state[h] * jnp.exp(glast[h]) + _tn_dot(k_dec[h], v_new[h]) for h in heads]
    for h in heads:
        s_ref[h] = s_new[h]
        o_ref[0, :, cols[h]] = o[h]


def _delta(qkvn, gc3, gt4, *, reverse):
    _, b, s, _ = qkvn.shape
    nc = s // DN_CHUNK
    if reverse:
        cidx = lambda i: nc - 1 - i
    else:
        cidx = lambda i: i
    return pl.pallas_call(
        functools.partial(_delta_kernel, reverse=reverse),
        grid=(b, nc),
        in_specs=[
            pl.BlockSpec((1, 1, DN_CHUNK, D_DN), lambda bb, i: (0, bb, cidx(i), 0)),
            pl.BlockSpec((1, 1, DN_CHUNK, D_DN), lambda bb, i: (1, bb, cidx(i), 0)),
            pl.BlockSpec((1, 1, DN_CHUNK, D_DN), lambda bb, i: (2, bb, cidx(i), 0)),
            pl.BlockSpec((1, DN_CHUNK, LANES), lambda bb, i: (bb, cidx(i), 0)),
            pl.BlockSpec((1, 1, N_GATE, DN_CHUNK), lambda bb, i: (bb, cidx(i), 0, 0)),
        ],
        out_specs=pl.BlockSpec((1, DN_CHUNK, D_DN), lambda bb, i: (bb, cidx(i), 0)),
        out_shape=jax.ShapeDtypeStruct((b, s, D_DN), F32),
        scratch_shapes=[pltpu.VMEM((DN_HEADS, DN_HEAD_DIM, DN_HEAD_DIM), F32)],
        compiler_params=pltpu.CompilerParams(
            dimension_semantics=("parallel", "arbitrary"),
            vmem_limit_bytes=VMEM_LIMIT),
        name="delta_bwd" if reverse else "delta_fwd",
    )(qkvn, qkvn, qkvn, gc3, gt4)


def _outproj_kernel(x_ref, ya_ref, of_ref, ob_ref, zb_ref, nw_ref, w_ref, fw_ref,
                    o_ref, y_ref, *, final):
    y_ref[:, 0:D_SGU] = ya_ref[...].astype(BF16)
    for h in range(DN_HEADS):
        cols = slice(h * DN_HEAD_DIM, (h + 1) * DN_HEAD_DIM)
        o = of_ref[:, cols] + ob_ref[:, cols]
        ms = jnp.mean(o * o, axis=-1, keepdims=True)
        z = zb_ref[:, cols]
        yh = o * lax.rsqrt(ms + NORM_EPS) * nw_ref[...] * (z * _sigmoid(z))
        y_ref[:, D_SGU + h * DN_HEAD_DIM:D_SGU + (h + 1) * DN_HEAD_DIM] = yh.astype(BF16)
    xn = x_ref[...] + jnp.dot(y_ref[...], w_ref[...], preferred_element_type=F32)
    if final:
        ms = jnp.mean(xn * xn, axis=-1, keepdims=True)
        xn = xn * lax.rsqrt(ms + NORM_EPS) * fw_ref[...]
    o_ref[...] = xn


def _outproj(x2, y_a, o_f, o_b, proj, dn_norm_w, w_out_bf16, final_w, *, final):
    t = x2.shape[0]
    return pl.pallas_call(
        functools.partial(_outproj_kernel, final=final),
        grid=(t // TM_OUT,),
        in_specs=[
            pl.BlockSpec((TM_OUT, D_MODEL), lambda i: (i, 0)),
            pl.BlockSpec((TM_OUT, D_SGU), lambda i: (i, 0)),
            pl.BlockSpec((TM_OUT, D_DN), lambda i: (i, 0)),
            pl.BlockSpec((TM_OUT, D_DN), lambda i: (i, 0)),
            pl.BlockSpec((TM_OUT, COL_BLOCK), lambda i: (i, CB_ZB)),
            pl.BlockSpec((1, DN_HEAD_DIM), lambda i: (0, 0)),
            pl.BlockSpec((D_MODEL, D_MODEL), lambda i: (0, 0)),
            pl.BlockSpec((1, D_MODEL), lambda i: (0, 0)),
        ],
        out_specs=pl.BlockSpec((TM_OUT, D_MODEL), lambda i: (i, 0)),
        out_shape=jax.ShapeDtypeStruct((t, D_MODEL), F32),
        scratch_shapes=[pltpu.VMEM((TM_OUT, D_MODEL), BF16)],
        compiler_params=pltpu.CompilerParams(
            dimension_semantics=("parallel",), vmem_limit_bytes=VMEM_LIMIT),
        name="outproj_final" if final else "outproj",
    )(x2, y_a, o_f, o_b, proj, dn_norm_w, w_out_bf16, final_w)


def _pad_lanes(row):
    return jnp.pad(row, (0, LANES - row.shape[0]))[None, :]


def kernel(x, norm_w, w_in, sgu_ln_g, sgu_ln_b, sgu_w, sgu_b, conv_w, a_log_f, a_log_b,
           dt_bias_f, dt_bias_b, dn_norm_w, w_out, final_norm_w):
    b, s, _ = x.shape
    t = b * s
    depth = w_in.shape[0]
    x2 = x.reshape(t, D_MODEL)
    for l in range(depth):
        w_main = w_in[l, :, :D_MAIN].astype(BF16)
        w_gate = jnp.pad(w_in[l, :, D_MAIN:], ((0, 0), (0, LANES - N_GATE)))
        proj, gates_raw = _inproj(x2, norm_w[l][None, :], w_main, w_gate)
        alog_row = _pad_lanes(jnp.concatenate([a_log_f[l], a_log_b[l]]))
        dtb_row = _pad_lanes(jnp.concatenate([dt_bias_f[l], dt_bias_b[l]]))
        gc, gt = _gates(gates_raw, alog_row, dtb_row)
        y_a = _sgu(proj, sgu_ln_g[l][None, :], sgu_ln_b[l][None, :],
                   sgu_w[l].astype(BF16), sgu_b[l].T)
        qkvn = _prep(proj.reshape(b, s, D_MAIN), conv_w[l])
        gc3 = gc.reshape(b, s, LANES)
        gt4 = gt.reshape(b, s // DN_CHUNK, N_GATE, DN_CHUNK)
        o_f = _delta(qkvn, gc3, gt4, reverse=False).reshape(t, D_DN)
        o_b = _delta(qkvn, gc3, gt4, reverse=True).reshape(t, D_DN)
        x2 = _outproj(x2, y_a, o_f, o_b, proj, dn_norm_w[l][None, :],
                      w_out[l].astype(BF16), final_norm_w[None, :],
                      final=(l == depth - 1))
    return x2.reshape(b, s, D_MODEL)
```

```python
import functools

import jax
import jax.numpy as jnp
from jax import lax
from jax.experimental import pallas as pl
from jax.experimental.pallas import tpu as pltpu

D_MODEL = 2048
D_SGU = 1024
SGU_GROUPS = 8
SGU_GROUP_DIM = 128
SGU_CHUNK = 128
D_DN = 1024
DN_HEADS = 8
DN_HEAD_DIM = 128
DN_CHUNK = 128
CONV_WIDTH = 5
CONV_PAD = CONV_WIDTH // 2
NORM_EPS = 1e-6
D_MAIN = 3 * D_SGU + 4 * D_DN
N_GATE = 4 * DN_HEADS
LANES = 128
SUBLANES = 8
BF16_ROWS = 16
COL_BLOCK = 1024

CB_U, CB_V, CB_ZA, CB_Q, CB_ZB = 0, 1, 2, 3, 6
GC_FWD, GC_BWD, BETA_FWD, BETA_BWD = 0, DN_HEADS, 2 * DN_HEADS, 3 * DN_HEADS

F32 = jnp.float32
BF16 = jnp.bfloat16
HIGHEST = lax.Precision.HIGHEST

TM_IN = 1024
TN_IN = 1792
R_GATE = 256
R_SGU = 256
R_PREP = 256
TM_OUT = 256
VMEM_LIMIT = 52 * 1024 * 1024


def _sigmoid(x):
    return 1.0 / (1.0 + jnp.exp(-x))


def _gelu(x):
    return 0.5 * x * (1.0 + lax.erf(x * (2.0 ** -0.5)))


def _mm(a, b):
    return jnp.dot(a.astype(BF16), b.astype(BF16), preferred_element_type=F32)


def _nt_dot(a, b):
    return lax.dot_general(a, b, (((1,), (1,)), ((), ())), preferred_element_type=F32)


def _tn_dot(a, b):
    return lax.dot_general(a, b, (((0,), (0,)), ((), ())), preferred_element_type=F32)


def _inproj_kernel(x_ref, nw_ref, w_ref, wg_ref, out_ref, gate_ref, hb_ref):
    @pl.when(pl.program_id(1) == 0)
    def _():
        x = x_ref[...]
        ms = jnp.mean(x * x, axis=-1, keepdims=True)
        h = x * lax.rsqrt(ms + NORM_EPS) * nw_ref[...]
        hb = h.astype(BF16)
        hb_ref[...] = hb
        h_lo = (h - hb.astype(F32)).astype(BF16)
        g_hi = jnp.dot(hb, wg_ref[0], preferred_element_type=F32)
        g_lo = jnp.dot(h_lo, wg_ref[0, :, 0:LANES], preferred_element_type=F32)
        gate_ref[...] = g_hi[:, 0:LANES] + g_hi[:, LANES:2 * LANES] + g_lo

    out_ref[...] = jnp.dot(hb_ref[...], w_ref[0],
                           preferred_element_type=F32).astype(BF16)


def _inproj(x2, norm_w, w_all, wg_all, layer):
    t = x2.shape[0]
    return pl.pallas_call(
        _inproj_kernel,
        grid=(t // TM_IN, D_MAIN // TN_IN),
        in_specs=[
            pl.BlockSpec((TM_IN, D_MODEL), lambda i, j: (i, 0)),
            pl.BlockSpec((1, D_MODEL), lambda i, j: (0, 0)),
            pl.BlockSpec((1, D_MODEL, TN_IN), lambda i, j: (layer, 0, j)),
            pl.BlockSpec((1, D_MODEL, 2 * LANES), lambda i, j: (layer, 0, 0)),
        ],
        out_specs=[
            pl.BlockSpec((TM_IN, TN_IN), lambda i, j: (i, j)),
            pl.BlockSpec((TM_IN, LANES), lambda i, j: (i, 0)),
        ],
        out_shape=[
            jax.ShapeDtypeStruct((t, D_MAIN), BF16),
            jax.ShapeDtypeStruct((t, LANES), F32),
        ],
        scratch_shapes=[pltpu.VMEM((TM_IN, D_MODEL), BF16)],
        compiler_params=pltpu.CompilerParams(
            dimension_semantics=("parallel", "arbitrary"),
            vmem_limit_bytes=VMEM_LIMIT),
        name="inproj",
    )(x2, norm_w, w_all, wg_all)


def _gate_kernel(g_ref, alog_ref, dtb_ref, gc_ref, gt_ref):
    r = g_ref.shape[0]
    raw = g_ref[...]
    lane = lax.broadcasted_iota(jnp.int32, raw.shape, 1)
    sp_in = raw + dtb_ref[...]
    softplus = jnp.maximum(sp_in, 0.0) + jnp.log1p(jnp.exp(-jnp.abs(sp_in)))
    g = jnp.where(lane < BETA_FWD, -jnp.exp(alog_ref[...]) * softplus, 0.0)
    beta = _sigmoid(raw)
    ri = lax.broadcasted_iota(jnp.int32, (r, r), 0)
    ci = lax.broadcasted_iota(jnp.int32, (r, r), 1)
    same = (ri // DN_CHUNK) == (ci // DN_CHUNK)
    lower = jnp.where(same & (ri >= ci), 1.0, 0.0).astype(F32)
    upper = jnp.where(same & (ri <= ci), 1.0, 0.0).astype(F32)
    pre = jnp.dot(lower, g, precision=HIGHEST, preferred_element_type=F32)
    suf = jnp.dot(upper, g, precision=HIGHEST, preferred_element_type=F32)
    out = jnp.where(lane < GC_BWD, pre,
                    jnp.where(lane < BETA_FWD, suf,
                              jnp.where(lane < N_GATE, beta, 0.0)))
    gc_ref[...] = out
    out_t = out.T
    for c in range(r // DN_CHUNK):
        gt_ref[c] = out_t[0:N_GATE, c * DN_CHUNK:(c + 1) * DN_CHUNK]


def _gates(gates_raw, alog_row, dtb_row):
    t = gates_raw.shape[0]
    return pl.pallas_call(
        _gate_kernel,
        grid=(t // R_GATE,),
        in_specs=[
            pl.BlockSpec((R_GATE, LANES), lambda i: (i, 0)),
            pl.BlockSpec((1, LANES), lambda i: (0, 0)),
            pl.BlockSpec((1, LANES), lambda i: (0, 0)),
        ],
        out_specs=[
            pl.BlockSpec((R_GATE, LANES), lambda i: (i, 0)),
            pl.BlockSpec((R_GATE // DN_CHUNK, N_GATE, DN_CHUNK), lambda i: (i, 0, 0)),
        ],
        out_shape=[
            jax.ShapeDtypeStruct((t, LANES), F32),
            jax.ShapeDtypeStruct((t // DN_CHUNK, N_GATE, DN_CHUNK), F32),
        ],
        compiler_params=pltpu.CompilerParams(dimension_semantics=("parallel",)),
        name="gates",
    )(gates_raw, alog_row, dtb_row)


def _sgu_kernel(u_ref, v_ref, z_ref, lg_ref, lb_ref, ws_ref, bs_ref, o_ref, vn_ref):
    r = u_ref.shape[0]
    v = _gelu(v_ref[...].astype(F32))
    mu = jnp.mean(v, axis=-1, keepdims=True)
    vc = v - mu
    var = jnp.mean(vc * vc, axis=-1, keepdims=True)
    vn = vc * lax.rsqrt(var + NORM_EPS) * lg_ref[...] + lb_ref[...]
    vn_ref[...] = vn.astype(BF16)
    for c in range(r // SGU_CHUNK):
        rows = slice(c * SGU_CHUNK, (c + 1) * SGU_CHUNK)
        for g in range(SGU_GROUPS):
            cols = slice(g * SGU_GROUP_DIM, (g + 1) * SGU_GROUP_DIM)
            sp = jnp.dot(ws_ref[0, g], vn_ref[rows, cols], preferred_element_type=F32)
            sp = sp + bs_ref[:, g:g + 1]
            u = _gelu(u_ref[rows, cols].astype(F32))
            z = z_ref[rows, cols].astype(F32)
            o_ref[rows, cols] = (u * sp * (z * _sigmoid(z))).astype(BF16)


def _sgu(proj, ln_g, ln_b, ws_all, bs_t, layer):
    t = proj.shape[0]
    return pl.pallas_call(
        _sgu_kernel,
        grid=(t // R_SGU,),
        in_specs=[
            pl.BlockSpec((R_SGU, COL_BLOCK), lambda i: (i, CB_U)),
            pl.BlockSpec((R_SGU, COL_BLOCK), lambda i: (i, CB_V)),
            pl.BlockSpec((R_SGU, COL_BLOCK), lambda i: (i, CB_ZA)),
            pl.BlockSpec((1, D_SGU), lambda i: (0, 0)),
            pl.BlockSpec((1, D_SGU), lambda i: (0, 0)),
            pl.BlockSpec((1, SGU_GROUPS, SGU_CHUNK, SGU_CHUNK), lambda i: (layer, 0, 0, 0)),
            pl.BlockSpec((SGU_CHUNK, SGU_GROUPS), lambda i: (0, 0)),
        ],
        out_specs=pl.BlockSpec((R_SGU, D_SGU), lambda i: (i, 0)),
        out_shape=jax.ShapeDtypeStruct((t, D_SGU), BF16),
        scratch_shapes=[pltpu.VMEM((R_SGU, D_SGU), BF16)],
        compiler_params=pltpu.CompilerParams(
            dimension_semantics=("parallel",), vmem_limit_bytes=VMEM_LIMIT),
        name="sgu",
    )(proj, proj, proj, ln_g, ln_b, ws_all, bs_t)


def _prep_kernel(x_ref, p_ref, n_ref, cw_ref, o_ref, pad_ref):
    kind = pl.program_id(0)
    i = pl.program_id(2)
    r = x_ref.shape[1]
    halo = BF16_ROWS
    pad_ref[0:halo, :] = jnp.where(i > 0, p_ref[0].astype(F32), 0.0)
    pad_ref[halo:halo + r, :] = x_ref[0].astype(F32)
    pad_ref[halo + r:2 * halo + r, :] = jnp.where(
        i < pl.num_programs(2) - 1, n_ref[0].astype(F32), 0.0)
    base = halo - CONV_PAD
    acc = cw_ref[0:1, :] * pad_ref[base:base + r, :]
    for j in range(1, CONV_WIDTH):
        acc = acc + cw_ref[j:j + 1, :] * pad_ref[base + j:base + j + r, :]
    y = acc * _sigmoid(acc)

    @pl.when(kind == 2)
    def _():
        o_ref[0, 0] = y

    @pl.when(kind < 2)
    def _():
        scale = jnp.where(kind == 0, DN_HEAD_DIM ** -0.5, 1.0).astype(F32)
        for h in range(DN_HEADS):
            cols = slice(h * DN_HEAD_DIM, (h + 1) * DN_HEAD_DIM)
            yh = y[:, cols]
            ss = jnp.sum(yh * yh, axis=-1, keepdims=True)
            o_ref[0, 0, :, cols] = yh * lax.rsqrt(ss + NORM_EPS) * scale


def _prep(proj3, conv_w):
    b, s, _ = proj3.shape
    nhalo = s // BF16_ROWS
    rb = R_PREP // BF16_ROWS
    return pl.pallas_call(
        _prep_kernel,
        grid=(3, b, s // R_PREP),
        in_specs=[
            pl.BlockSpec((1, R_PREP, COL_BLOCK), lambda kd, bb, i: (bb, i, CB_Q + kd)),
            pl.BlockSpec((1, BF16_ROWS, COL_BLOCK),
                         lambda kd, bb, i: (bb, jnp.maximum(i * rb - 1, 0), CB_Q + kd)),
            pl.BlockSpec((1, BF16_ROWS, COL_BLOCK),
                         lambda kd, bb, i: (bb, jnp.minimum((i + 1) * rb, nhalo - 1), CB_Q + kd)),
            pl.BlockSpec((CONV_WIDTH, COL_BLOCK), lambda kd, bb, i: (0, kd)),
        ],
        out_specs=pl.BlockSpec((1, 1, R_PREP, D_DN), lambda kd, bb, i: (kd, bb, i, 0)),
        out_shape=jax.ShapeDtypeStruct((3, b, s, D_DN), F32),
        scratch_shapes=[pltpu.VMEM((R_PREP + 2 * BF16_ROWS, COL_BLOCK), F32)],
        compiler_params=pltpu.CompilerParams(
            dimension_semantics=("parallel", "parallel", "parallel"),
            vmem_limit_bytes=VMEM_LIMIT),
        name="prep",
    )(proj3, proj3, proj3, conv_w)


def _tri_inverse(ms, eye, ri, ci):
    blk = SUBLANES
    same = (ri // blk) == (ci // blk)
    p = [jnp.where(same, m, 0.0).astype(BF16) for m in ms]
    d = [eye - x for x in p]
    p = [_mm(x, x).astype(BF16) for x in p]
    d = [x + _mm(x, y) for x, y in zip(d, p)]
    p = [_mm(x, x).astype(BF16) for x in p]
    d = [x + _mm(x, y) for x, y in zip(d, p)]
    while blk < DN_CHUNK:
        same2 = (ri // (2 * blk)) == (ci // (2 * blk))
        sel = same2 & jnp.logical_not(same)
        off = [jnp.where(sel, m, 0.0).astype(BF16) for m in ms]
        db = [x.astype(BF16) for x in d]
        t = [_mm(x, y) for x, y in zip(db, off)]
        d = [x - _mm(y, z) for x, y, z in zip(d, t, db)]
        same = same2
        blk *= 2
    return d


def _delta_kernel(qf_ref, kf_ref, vf_ref, gcf_ref, gtf_ref,
                  qb_ref, kb_ref, vb_ref, gcb_ref, gtb_ref,
                  of_ref, ob_ref, s_ref):
    @pl.when(pl.program_id(1) == 0)
    def _():
        s_ref[...] = jnp.zeros_like(s_ref)

    c = DN_CHUNK
    ri = lax.broadcasted_iota(jnp.int32, (c, c), 0)
    ci = lax.broadcasted_iota(jnp.int32, (c, c), 1)
    eye = jnp.where(ri == ci, 1.0, 0.0).astype(F32)
    fwd = dict(q=qf_ref, k=kf_ref, v=vf_ref, gc=gcf_ref[0], gt=gtf_ref[0, 0], o=of_ref,
               incl=ri >= ci, strict=ri > ci, g_off=GC_FWD, b_off=BETA_FWD, last=c - 1, s0=0)
    bwd = dict(q=qb_ref, k=kb_ref, v=vb_ref, gc=gcb_ref[0], gt=gtb_ref[0, 0], o=ob_ref,
               incl=ri <= ci, strict=ri < ci, g_off=GC_BWD, b_off=BETA_BWD, last=0,
               s0=DN_HEADS)
    probs = [(d, h) for d in (fwd, bwd) for h in range(DN_HEADS)]
    n = range(len(probs))
    cols = [slice(h * DN_HEAD_DIM, (h + 1) * DN_HEAD_DIM) for _, h in probs]
    q = [d["q"][0, 0, :, cols[i]] for i, (d, h) in enumerate(probs)]
    k = [d["k"][0, 0, :, cols[i]] for i, (d, h) in enumerate(probs)]
    v = [d["v"][0, 0, :, cols[i]] for i, (d, h) in enumerate(probs)]
    gcol = [d["gc"][:, d["g_off"] + h:d["g_off"] + h + 1] for d, h in probs]
    bcol = [d["gc"][:, d["b_off"] + h:d["b_off"] + h + 1] for d, h in probs]
    grow = [d["gt"][d["g_off"] + h:d["g_off"] + h + 1, :] for d, h in probs]
    glast = [grow[i][:, d["last"]:d["last"] + 1] for i, (d, h) in enumerate(probs)]
    kbeta = [k[i] * bcol[i] for i in n]
    kb = [k[i].astype(BF16) for i in n]
    kk = [_nt_dot(kbeta[i].astype(BF16), kb[i]) for i in n]
    qk = [_nt_dot(q[i].astype(BF16), kb[i]) for i in n]
    decay = [jnp.where(d["incl"], jnp.exp(gcol[i] - grow[i]), 0.0)
             for i, (d, h) in enumerate(probs)]
    m = [jnp.where(d["strict"], kk[i] * decay[i], 0.0) for i, (d, h) in enumerate(probs)]
    t_inv = _tri_inverse(m, eye, ri, ci)
    eg = [jnp.exp(gcol[i]) for i in n]
    rhs = [jnp.concatenate([v[i] * bcol[i], kbeta[i] * eg[i]], axis=1) for i in n]
    uw = [_mm(t_inv[i], rhs[i]) for i in n]
    attn = [(qk[i] * decay[i]).astype(BF16) for i in n]
    lhs = [jnp.concatenate([uw[i][:, DN_HEAD_DIM:], q[i] * eg[i]], axis=0) for i in n]
    state = [s_ref[d["s0"] + h] for d, h in probs]
    ws = [_mm(lhs[i], state[i]) for i in n]
    v_new = [(uw[i][:, :DN_HEAD_DIM] - ws[i][:c]).astype(BF16) for i in n]
    k_dec = [(k[i] * jnp.exp(glast[i] - gcol[i])).astype(BF16) for i in n]
    o = [ws[i][c:] + jnp.dot(attn[i], v_new[i], preferred_element_type=F32) for i in n]
    s_new = [state[i] * jnp.exp(glast[i]) + _tn_dot(k_dec[i], v_new[i]) for i in n]
    for i, (d, h) in enumerate(probs):
        s_ref[d["s0"] + h] = s_new[i]
        d["o"][0, :, cols[i]] = o[i]


def _delta(qkvn, gc3, gt4):
    _, b, s, _ = qkvn.shape
    nc = s // DN_CHUNK

    def dir_specs(cidx):
        return [
            pl.BlockSpec((1, 1, DN_CHUNK, D_DN), lambda bb, i: (0, bb, cidx(i), 0)),
            pl.BlockSpec((1, 1, DN_CHUNK, D_DN), lambda bb, i: (1, bb, cidx(i), 0)),
            pl.BlockSpec((1, 1, DN_CHUNK, D_DN), lambda bb, i: (2, bb, cidx(i), 0)),
            pl.BlockSpec((1, DN_CHUNK, LANES), lambda bb, i: (bb, cidx(i), 0)),
            pl.BlockSpec((1, 1, N_GATE, DN_CHUNK), lambda bb, i: (bb, cidx(i), 0, 0)),
        ]

    up = lambda i: i
    down = lambda i: nc - 1 - i
    return pl.pallas_call(
        _delta_kernel,
        grid=(b, nc),
        in_specs=dir_specs(up) + dir_specs(down),
        out_specs=[
            pl.BlockSpec((1, DN_CHUNK, D_DN), lambda bb, i: (bb, up(i), 0)),
            pl.BlockSpec((1, DN_CHUNK, D_DN), lambda bb, i: (bb, down(i), 0)),
        ],
        out_shape=[jax.ShapeDtypeStruct((b, s, D_DN), F32)] * 2,
        scratch_shapes=[pltpu.VMEM((2 * DN_HEADS, DN_HEAD_DIM, DN_HEAD_DIM), F32)],
        compiler_params=pltpu.CompilerParams(
            dimension_semantics=("parallel", "arbitrary"),
            vmem_limit_bytes=VMEM_LIMIT),
        name="delta",
    )(qkvn, qkvn, qkvn, gc3, gt4, qkvn, qkvn, qkvn, gc3, gt4)


def _outproj_kernel(x_ref, ya_ref, of_ref, ob_ref, zb_ref, nw_ref, w_ref, fw_ref,
                    o_ref, y_ref, *, final):
    y_ref[:, 0:D_SGU] = ya_ref[...]
    for h in range(DN_HEADS):
        cols = slice(h * DN_HEAD_DIM, (h + 1) * DN_HEAD_DIM)
        o = of_ref[:, cols] + ob_ref[:, cols]
        ms = jnp.mean(o * o, axis=-1, keepdims=True)
        z = zb_ref[:, cols].astype(F32)
        yh = o * lax.rsqrt(ms + NORM_EPS) * nw_ref[...] * (z * _sigmoid(z))
        y_ref[:, D_SGU + h * DN_HEAD_DIM:D_SGU + (h + 1) * DN_HEAD_DIM] = yh.astype(BF16)
    xn = x_ref[...] + jnp.dot(y_ref[...], w_ref[0], preferred_element_type=F32)
    if final:
        ms = jnp.mean(xn * xn, axis=-1, keepdims=True)
        xn = xn * lax.rsqrt(ms + NORM_EPS) * fw_ref[...]
    o_ref[...] = xn


def _outproj(x2, y_a, o_f, o_b, proj, dn_norm_w, w_out_all, final_w, layer, *, final):
    t = x2.shape[0]
    return pl.pallas_call(
        functools.partial(_outproj_kernel, final=final),
        grid=(t // TM_OUT,),
        in_specs=[
            pl.BlockSpec((TM_OUT, D_MODEL), lambda i: (i, 0)),
            pl.BlockSpec((TM_OUT, D_SGU), lambda i: (i, 0)),
            pl.BlockSpec((TM_OUT, D_DN), lambda i: (i, 0)),
            pl.BlockSpec((TM_OUT, D_DN), lambda i: (i, 0)),
            pl.BlockSpec((TM_OUT, COL_BLOCK), lambda i: (i, CB_ZB)),
            pl.BlockSpec((1, DN_HEAD_DIM), lambda i: (0, 0)),
            pl.BlockSpec((1, D_MODEL, D_MODEL), lambda i: (layer, 0, 0)),
            pl.BlockSpec((1, D_MODEL), lambda i: (0, 0)),
        ],
        out_specs=pl.BlockSpec((TM_OUT, D_MODEL), lambda i: (i, 0)),
        out_shape=jax.ShapeDtypeStruct((t, D_MODEL), F32),
        scratch_shapes=[pltpu.VMEM((TM_OUT, D_MODEL), BF16)],
        compiler_params=pltpu.CompilerParams(
            dimension_semantics=("parallel",), vmem_limit_bytes=VMEM_LIMIT),
        name="outproj_final" if final else "outproj",
    )(x2, y_a, o_f, o_b, proj, dn_norm_w, w_out_all, final_w)


def _pad_lanes(row):
    return jnp.pad(row, (0, LANES - row.shape[0]))[None, :]


def kernel(x, norm_w, w_in, sgu_ln_g, sgu_ln_b, sgu_w, sgu_b, conv_w, a_log_f, a_log_b,
           dt_bias_f, dt_bias_b, dn_norm_w, w_out, final_norm_w):
    b, s, _ = x.shape
    t = b * s
    depth = w_in.shape[0]
    x2 = x.reshape(t, D_MODEL)
    w_all = w_in.astype(BF16)
    w_gate = jnp.pad(w_in[:, :, D_MAIN:], ((0, 0), (0, 0), (0, LANES - N_GATE)))
    wg_hi = w_gate.astype(BF16)
    wg_lo = (w_gate - wg_hi.astype(F32)).astype(BF16)
    wg_all = jnp.concatenate([wg_hi, wg_lo], axis=-1)
    ws_all = sgu_w.astype(BF16)
    w_out_all = w_out.astype(BF16)
    for l in range(depth):
        proj, gates_raw = _inproj(x2, norm_w[l][None, :], w_all, wg_all, l)
        alog_row = _pad_lanes(jnp.concatenate([a_log_f[l], a_log_b[l]]))
        dtb_row = _pad_lanes(jnp.concatenate([dt_bias_f[l], dt_bias_b[l]]))
        gc, gt = _gates(gates_raw, alog_row, dtb_row)
        y_a = _sgu(proj, sgu_ln_g[l][None, :], sgu_ln_b[l][None, :], ws_all, sgu_b[l].T, l)
        qkvn = _prep(proj.reshape(b, s, D_MAIN), conv_w[l])
        gc3 = gc.reshape(b, s, LANES)
        gt4 = gt.reshape(b, s // DN_CHUNK, N_GATE, DN_CHUNK)
        o_f, o_b = _delta(qkvn, gc3, gt4)
        x2 = _outproj(x2, y_a, o_f.reshape(t, D_DN), o_b.reshape(t, D_DN), proj,
                      dn_norm_w[l][None, :], w_out_all, final_norm_w[None, :], l,
                      final=(l == depth - 1))
    return x2.reshape(b, s, D_MODEL)
```

```python
import functools

import jax
import jax.numpy as jnp
from jax import lax
from jax.experimental import pallas as pl
from jax.experimental.pallas import tpu as pltpu

D_MODEL = 2048
D_SGU = 1024
SGU_GROUPS = 8
SGU_GROUP_DIM = 128
SGU_CHUNK = 128
D_DN = 1024
DN_HEADS = 8
DN_HEAD_DIM = 128
DN_CHUNK = 128
CONV_WIDTH = 5
CONV_PAD = CONV_WIDTH // 2
NORM_EPS = 1e-6
D_MAIN = 3 * D_SGU + 4 * D_DN
N_GATE = 4 * DN_HEADS
LANES = 128
SUBLANES = 8
BF16_ROWS = 16
COL_BLOCK = 1024

CB_U, CB_V, CB_ZA, CB_Q, CB_ZB = 0, 1, 2, 3, 6
GC_FWD, GC_BWD, BETA_FWD, BETA_BWD = 0, DN_HEADS, 2 * DN_HEADS, 3 * DN_HEADS

F32 = jnp.float32
BF16 = jnp.bfloat16
HIGHEST = lax.Precision.HIGHEST

TM_IN = 1024
TN_IN = 1792
R_GATE = 128
R_SGU = 256
R_PREP = 256
PREP_TILE = 128
TM_OUT = 512
DELTA_NCH = 2
R_CAST = 256
VMEM_LIMIT = 52 * 1024 * 1024


def _sigmoid(x):
    return 1.0 / (1.0 + jnp.exp(-x))


def _gelu(x):
    return 0.5 * x * (1.0 + lax.erf(x * (2.0 ** -0.5)))


def _mm(a, b):
    return jnp.dot(a.astype(BF16), b.astype(BF16), preferred_element_type=F32)


def _nt_dot(a, b):
    return lax.dot_general(a, b, (((1,), (1,)), ((), ())), preferred_element_type=F32)


def _tn_dot(a, b):
    return lax.dot_general(a, b, (((0,), (0,)), ((), ())), preferred_element_type=F32)


def _inproj_kernel(x_ref, nw_ref, w_ref, wg_ref, out_ref, gate_ref, hb_ref):
    @pl.when(pl.program_id(1) == 0)
    def _():
        x = x_ref[...]
        ms = jnp.mean(x * x, axis=-1, keepdims=True)
        h = x * lax.rsqrt(ms + NORM_EPS) * nw_ref[...]
        hb = h.astype(BF16)
        hb_ref[...] = hb
        h_lo = (h - hb.astype(F32)).astype(BF16)
        g_hi = jnp.dot(hb, wg_ref[0], preferred_element_type=F32)
        g_lo = jnp.dot(h_lo, wg_ref[0, :, 0:LANES], preferred_element_type=F32)
        gate_ref[...] = g_hi[:, 0:LANES] + g_hi[:, LANES:2 * LANES] + g_lo

    out_ref[...] = jnp.dot(hb_ref[...], w_ref[0],
                           preferred_element_type=F32).astype(BF16)


def _inproj(x2, norm_w, w_all, wg_all, layer):
    t = x2.shape[0]
    return pl.pallas_call(
        _inproj_kernel,
        grid=(t // TM_IN, D_MAIN // TN_IN),
        in_specs=[
            pl.BlockSpec((TM_IN, D_MODEL), lambda i, j: (i, 0)),
            pl.BlockSpec((1, D_MODEL), lambda i, j: (0, 0)),
            pl.BlockSpec((1, D_MODEL, TN_IN), lambda i, j: (layer, 0, j)),
            pl.BlockSpec((1, D_MODEL, 2 * LANES), lambda i, j: (layer, 0, 0)),
        ],
        out_specs=[
            pl.BlockSpec((TM_IN, TN_IN), lambda i, j: (i, j)),
            pl.BlockSpec((TM_IN, LANES), lambda i, j: (i, 0)),
        ],
        out_shape=[
            jax.ShapeDtypeStruct((t, D_MAIN), BF16),
            jax.ShapeDtypeStruct((t, LANES), F32),
        ],
        scratch_shapes=[pltpu.VMEM((TM_IN, D_MODEL), BF16)],
        compiler_params=pltpu.CompilerParams(
            dimension_semantics=("parallel", "arbitrary"),
            vmem_limit_bytes=VMEM_LIMIT),
        name="inproj",
    )(x2, norm_w, w_all, wg_all)


def _gate_kernel(g_ref, alog_ref, dtb_ref, gc_ref, gt_ref):
    r = g_ref.shape[0]
    raw = g_ref[...]
    lane = lax.broadcasted_iota(jnp.int32, raw.shape, 1)
    sp_in = raw + dtb_ref[...]
    softplus = jnp.maximum(sp_in, 0.0) + jnp.log1p(jnp.exp(-jnp.abs(sp_in)))
    g = jnp.where(lane < BETA_FWD, -jnp.exp(alog_ref[...]) * softplus, 0.0)
    beta = _sigmoid(raw)
    ri = lax.broadcasted_iota(jnp.int32, (r, r), 0)
    ci = lax.broadcasted_iota(jnp.int32, (r, r), 1)
    same = (ri // DN_CHUNK) == (ci // DN_CHUNK)
    lower = jnp.where(same & (ri >= ci), 1.0, 0.0).astype(F32)
    upper = jnp.where(same & (ri <= ci), 1.0, 0.0).astype(F32)
    pre = jnp.dot(lower, g, precision=HIGHEST, preferred_element_type=F32)
    suf = jnp.dot(upper, g, precision=HIGHEST, preferred_element_type=F32)
    out = jnp.where(lane < GC_BWD, pre,
                    jnp.where(lane < BETA_FWD, suf,
                              jnp.where(lane < N_GATE, beta, 0.0)))
    gc_ref[...] = out
    out_t = out.T
    for c in range(r // DN_CHUNK):
        gt_ref[c] = out_t[0:N_GATE, c * DN_CHUNK:(c + 1) * DN_CHUNK]


def _gates(gates_raw, alog_row, dtb_row):
    t = gates_raw.shape[0]
    return pl.pallas_call(
        _gate_kernel,
        grid=(t // R_GATE,),
        in_specs=[
            pl.BlockSpec((R_GATE, LANES), lambda i: (i, 0)),
            pl.BlockSpec((1, LANES), lambda i: (0, 0)),
            pl.BlockSpec((1, LANES), lambda i: (0, 0)),
        ],
        out_specs=[
            pl.BlockSpec((R_GATE, LANES), lambda i: (i, 0)),
            pl.BlockSpec((R_GATE // DN_CHUNK, N_GATE, DN_CHUNK), lambda i: (i, 0, 0)),
        ],
        out_shape=[
            jax.ShapeDtypeStruct((t, LANES), F32),
            jax.ShapeDtypeStruct((t // DN_CHUNK, N_GATE, DN_CHUNK), F32),
        ],
        compiler_params=pltpu.CompilerParams(dimension_semantics=("parallel",)),
        name="gates",
    )(gates_raw, alog_row, dtb_row)


def _sgu_kernel(u_ref, v_ref, z_ref, lg_ref, lb_ref, ws_ref, bs_ref, o_ref, vn_ref):
    r = u_ref.shape[0]
    v = _gelu(v_ref[...].astype(F32))
    mu = jnp.mean(v, axis=-1, keepdims=True)
    vc = v - mu
    var = jnp.mean(vc * vc, axis=-1, keepdims=True)
    vn = vc * lax.rsqrt(var + NORM_EPS) * lg_ref[...] + lb_ref[...]
    vn_ref[...] = vn.astype(BF16)
    for c in range(r // SGU_CHUNK):
        rows = slice(c * SGU_CHUNK, (c + 1) * SGU_CHUNK)
        for g in range(SGU_GROUPS):
            cols = slice(g * SGU_GROUP_DIM, (g + 1) * SGU_GROUP_DIM)
            sp = jnp.dot(ws_ref[0, g], vn_ref[rows, cols], preferred_element_type=F32)
            sp = sp + bs_ref[:, g:g + 1]
            u = _gelu(u_ref[rows, cols].astype(F32))
            z = z_ref[rows, cols].astype(F32)
            o_ref[rows, cols] = (u * sp * (z * _sigmoid(z))).astype(BF16)


def _sgu(proj, ln_g, ln_b, ws_all, bs_t, layer):
    t = proj.shape[0]
    return pl.pallas_call(
        _sgu_kernel,
        grid=(t // R_SGU,),
        in_specs=[
            pl.BlockSpec((R_SGU, COL_BLOCK), lambda i: (i, CB_U)),
            pl.BlockSpec((R_SGU, COL_BLOCK), lambda i: (i, CB_V)),
            pl.BlockSpec((R_SGU, COL_BLOCK), lambda i: (i, CB_ZA)),
            pl.BlockSpec((1, D_SGU), lambda i: (0, 0)),
            pl.BlockSpec((1, D_SGU), lambda i: (0, 0)),
            pl.BlockSpec((1, SGU_GROUPS, SGU_CHUNK, SGU_CHUNK), lambda i: (layer, 0, 0, 0)),
            pl.BlockSpec((SGU_CHUNK, SGU_GROUPS), lambda i: (0, 0)),
        ],
        out_specs=pl.BlockSpec((R_SGU, D_SGU), lambda i: (i, 0)),
        out_shape=jax.ShapeDtypeStruct((t, D_SGU), BF16),
        scratch_shapes=[pltpu.VMEM((R_SGU, D_SGU), BF16)],
        compiler_params=pltpu.CompilerParams(
            dimension_semantics=("parallel",), vmem_limit_bytes=VMEM_LIMIT),
        name="sgu",
    )(proj, proj, proj, ln_g, ln_b, ws_all, bs_t)


def _prep_kernel(x_ref, p_ref, n_ref, cw_ref, o_ref, pad_ref, *, normalize):
    kind = pl.program_id(0)
    i = pl.program_id(2)
    r = x_ref.shape[1]
    halo = BF16_ROWS
    zeros = jnp.zeros((halo, COL_BLOCK), BF16)
    pad_ref[0:halo, :] = jnp.where(i > 0, p_ref[0], zeros)
    pad_ref[halo:halo + r, :] = x_ref[0]
    pad_ref[halo + r:2 * halo + r, :] = jnp.where(i < pl.num_programs(2) - 1, n_ref[0], zeros)
    tile = PREP_TILE
    kdim = tile + 2 * halo
    ri = lax.broadcasted_iota(jnp.int32, (tile, kdim), 0)
    ci = lax.broadcasted_iota(jnp.int32, (tile, kdim), 1)
    taps = [j for j in range(CONV_WIDTH) if j != CONV_PAD]
    shift = jnp.concatenate(
        [jnp.where(ci == ri + halo + (j - CONV_PAD), 1.0, 0.0).astype(BF16) for j in taps],
        axis=0)
    scale = jnp.where(kind == 0, DN_HEAD_DIM ** -0.5, 1.0).astype(F32)
    for a in range(r // tile):
        rows = slice(a * tile, (a + 1) * tile)
        xs = pad_ref[a * tile:a * tile + kdim, :]
        shifted = jnp.dot(shift, xs, preferred_element_type=F32)
        acc = cw_ref[CONV_PAD:CONV_PAD + 1, :] * xs[halo:halo + tile, :].astype(F32)
        for n, j in enumerate(taps):
            acc = acc + cw_ref[j:j + 1, :] * shifted[n * tile:(n + 1) * tile, :]
        y = acc * _sigmoid(acc)
        if not normalize:
            o_ref[0, 0, rows, :] = y
            continue
        for h in range(DN_HEADS):
            cols = slice(h * DN_HEAD_DIM, (h + 1) * DN_HEAD_DIM)
            yh = y[:, cols]
            ss = jnp.sum(yh * yh, axis=-1, keepdims=True)
            o_ref[0, 0, rows, cols] = yh * lax.rsqrt(ss + NORM_EPS) * scale


def _prep(proj3, conv_w, *, first_block, n_blocks, normalize):
    b, s, _ = proj3.shape
    nhalo = s // BF16_ROWS
    rb = R_PREP // BF16_ROWS
    cw0 = first_block - CB_Q
    return pl.pallas_call(
        functools.partial(_prep_kernel, normalize=normalize),
        grid=(n_blocks, b, s // R_PREP),
        in_specs=[
            pl.BlockSpec((1, R_PREP, COL_BLOCK), lambda kd, bb, i: (bb, i, first_block + kd)),
            pl.BlockSpec((1, BF16_ROWS, COL_BLOCK),
                         lambda kd, bb, i: (bb, jnp.maximum(i * rb - 1, 0), first_block + kd)),
            pl.BlockSpec((1, BF16_ROWS, COL_BLOCK),
                         lambda kd, bb, i: (bb, jnp.minimum((i + 1) * rb, nhalo - 1),
                                            first_block + kd)),
            pl.BlockSpec((CONV_WIDTH, COL_BLOCK), lambda kd, bb, i: (0, cw0 + kd)),
        ],
        out_specs=pl.BlockSpec((1, 1, R_PREP, D_DN), lambda kd, bb, i: (kd, bb, i, 0)),
        out_shape=jax.ShapeDtypeStruct((n_blocks, b, s, D_DN), F32),
        scratch_shapes=[pltpu.VMEM((R_PREP + 2 * BF16_ROWS, COL_BLOCK), BF16)],
        compiler_params=pltpu.CompilerParams(
            dimension_semantics=("parallel", "parallel", "parallel"),
            vmem_limit_bytes=VMEM_LIMIT),
        name="prep_qk" if normalize else "prep_v",
    )(proj3, proj3, proj3, conv_w)


def _tri_inverse(ms, eye, ri, ci):
    blk = SUBLANES
    same = (ri // blk) == (ci // blk)
    p = [jnp.where(same, m, 0.0).astype(BF16) for m in ms]
    p2 = [_mm(x, x) for x in p]
    p2b = [x.astype(BF16) for x in p2]
    p34 = [_mm(y, jnp.concatenate([x, y], axis=1)) for x, y in zip(p, p2b)]
    d = [eye - x + y - z[:, :DN_CHUNK] for x, y, z in zip(p, p2, p34)]
    d = [x + _mm(z[:, DN_CHUNK:], x) for x, z in zip(d, p34)]
    while blk < DN_CHUNK:
        same2 = (ri // (2 * blk)) == (ci // (2 * blk))
        sel = same2 & jnp.logical_not(same)
        off = [jnp.where(sel, m, 0.0).astype(BF16) for m in ms]
        db = [x.astype(BF16) for x in d]
        t = [_mm(x, y) for x, y in zip(db, off)]
        d = [x - _mm(y, z) for x, y, z in zip(d, t, db)]
        same = same2
        blk *= 2
    return d


def _delta_kernel(qf_ref, kf_ref, vf_ref, gcf_ref, gtf_ref,
                  qb_ref, kb_ref, vb_ref, gcb_ref, gtb_ref,
                  of_ref, ob_ref, s_ref):
    @pl.when(pl.program_id(1) == 0)
    def _():
        s_ref[...] = jnp.zeros_like(s_ref)

    c = DN_CHUNK
    ri = lax.broadcasted_iota(jnp.int32, (c, c), 0)
    ci = lax.broadcasted_iota(jnp.int32, (c, c), 1)
    eye = jnp.where(ri == ci, 1.0, 0.0).astype(F32)
    fwd = dict(q=qf_ref, k=kf_ref, v=vf_ref, gc=gcf_ref, gt=gtf_ref, o=of_ref,
               incl=ri >= ci, strict=ri > ci, g_off=GC_FWD, b_off=BETA_FWD, last=c - 1,
               s0=0, order=list(range(DELTA_NCH)))
    bwd = dict(q=qb_ref, k=kb_ref, v=vb_ref, gc=gcb_ref, gt=gtb_ref, o=ob_ref,
               incl=ri <= ci, strict=ri < ci, g_off=GC_BWD, b_off=BETA_BWD, last=0,
               s0=DN_HEADS, order=list(range(DELTA_NCH - 1, -1, -1)))
    probs = [(d, t, h) for t in range(DELTA_NCH) for d in (fwd, bwd) for h in range(DN_HEADS)]
    n = range(len(probs))
    rows = [slice(d["order"][t] * c, (d["order"][t] + 1) * c) for d, t, h in probs]
    cols = [slice(h * DN_HEAD_DIM, (h + 1) * DN_HEAD_DIM) for d, t, h in probs]
    q = [d["q"][0, 0, rows[i], cols[i]] for i, (d, t, h) in enumerate(probs)]
    k = [d["k"][0, 0, rows[i], cols[i]] for i, (d, t, h) in enumerate(probs)]
    v = [d["v"][0, 0, rows[i], cols[i]] for i, (d, t, h) in enumerate(probs)]
    gcol = [d["gc"][0, rows[i], d["g_off"] + h:d["g_off"] + h + 1]
            for i, (d, t, h) in enumerate(probs)]
    bcol = [d["gc"][0, rows[i], d["b_off"] + h:d["b_off"] + h + 1]
            for i, (d, t, h) in enumerate(probs)]
    grow = [d["gt"][0, d["order"][t], d["g_off"] + h:d["g_off"] + h + 1, :]
            for d, t, h in probs]
    glast = [grow[i][:, d["last"]:d["last"] + 1] for i, (d, t, h) in enumerate(probs)]
    kbeta = [k[i] * bcol[i] for i in n]
    kb = [k[i].astype(BF16) for i in n]
    kq = [_nt_dot(jnp.concatenate([kbeta[i], q[i]], axis=0).astype(BF16), kb[i])
          for i in n]
    kk = [kq[i][:c] for i in n]
    qk = [kq[i][c:] for i in n]
    decay = [jnp.where(d["incl"], jnp.exp(gcol[i] - grow[i]), 0.0)
             for i, (d, t, h) in enumerate(probs)]
    m = [jnp.where(d["strict"], kk[i] * decay[i], 0.0) for i, (d, t, h) in enumerate(probs)]
    t_inv = _tri_inverse(m, eye, ri, ci)
    eg = [jnp.exp(gcol[i]) for i in n]
    rhs = [jnp.concatenate([v[i] * bcol[i], kbeta[i] * eg[i]], axis=1) for i in n]
    uw = [_mm(t_inv[i], rhs[i]) for i in n]
    attn = [(qk[i] * decay[i]).astype(BF16) for i in n]
    lhs = [jnp.concatenate([uw[i][:, DN_HEAD_DIM:], q[i] * eg[i]], axis=0).astype(BF16)
           for i in n]
    k_dec = [(k[i] * jnp.exp(glast[i] - gcol[i])).astype(BF16) for i in n]
    g_chunk = [jnp.exp(glast[i]) for i in n]
    state = {(d["s0"], h): s_ref[d["s0"] + h] for d in (fwd, bwd) for h in range(DN_HEADS)}
    for step in range(DELTA_NCH):
        cur = [i for i, (d, t, h) in enumerate(probs) if t == step]
        key = {i: (probs[i][0]["s0"], probs[i][2]) for i in cur}
        ws = {i: _mm(lhs[i], state[key[i]]) for i in cur}
        v_new = {i: (uw[i][:, :DN_HEAD_DIM] - ws[i][:c]).astype(BF16) for i in cur}
        o = {i: ws[i][c:] + jnp.dot(attn[i], v_new[i], preferred_element_type=F32)
             for i in cur}
        s_new = {i: state[key[i]] * g_chunk[i] + _tn_dot(k_dec[i], v_new[i]) for i in cur}
        for i in cur:
            state[key[i]] = s_new[i]
            probs[i][0]["o"][0, rows[i], cols[i]] = o[i].astype(BF16)
    for (s0, h), val in state.items():
        s_ref[s0 + h] = val


def _delta(qkn, vc, gc3, gt4):
    _, b, s, _ = qkn.shape
    r = DELTA_NCH * DN_CHUNK
    nb = s // r

    def dir_specs(bidx):
        return [
            pl.BlockSpec((1, 1, r, D_DN), lambda bb, i: (0, bb, bidx(i), 0)),
            pl.BlockSpec((1, 1, r, D_DN), lambda bb, i: (1, bb, bidx(i), 0)),
            pl.BlockSpec((1, 1, r, D_DN), lambda bb, i: (0, bb, bidx(i), 0)),
            pl.BlockSpec((1, r, LANES), lambda bb, i: (bb, bidx(i), 0)),
            pl.BlockSpec((1, DELTA_NCH, N_GATE, DN_CHUNK), lambda bb, i: (bb, bidx(i), 0, 0)),
        ]

    up = lambda i: i
    down = lambda i: nb - 1 - i
    return pl.pallas_call(
        _delta_kernel,
        grid=(b, nb),
        in_specs=dir_specs(up) + dir_specs(down),
        out_specs=[
            pl.BlockSpec((1, r, D_DN), lambda bb, i: (bb, up(i), 0)),
            pl.BlockSpec((1, r, D_DN), lambda bb, i: (bb, down(i), 0)),
        ],
        out_shape=[jax.ShapeDtypeStruct((b, s, D_DN), BF16)] * 2,
        scratch_shapes=[pltpu.VMEM((2 * DN_HEADS, DN_HEAD_DIM, DN_HEAD_DIM), F32)],
        compiler_params=pltpu.CompilerParams(
            dimension_semantics=("parallel", "arbitrary"),
            vmem_limit_bytes=VMEM_LIMIT),
        name="delta",
    )(qkn, qkn, vc, gc3, gt4, qkn, qkn, vc, gc3, gt4)


def _outproj_kernel(x_ref, ya_ref, of_ref, ob_ref, zb_ref, nw_ref, w_ref, fw_ref,
                    o_ref, y_ref, *, final):
    y_ref[:, 0:D_SGU] = ya_ref[...]
    for h in range(DN_HEADS):
        cols = slice(h * DN_HEAD_DIM, (h + 1) * DN_HEAD_DIM)
        o = of_ref[:, cols].astype(F32) + ob_ref[:, cols].astype(F32)
        ms = jnp.mean(o * o, axis=-1, keepdims=True)
        z = zb_ref[:, cols].astype(F32)
        yh = o * lax.rsqrt(ms + NORM_EPS) * nw_ref[...] * (z * _sigmoid(z))
        y_ref[:, D_SGU + h * DN_HEAD_DIM:D_SGU + (h + 1) * DN_HEAD_DIM] = yh.astype(BF16)
    xn = x_ref[...] + jnp.dot(y_ref[...], w_ref[0], preferred_element_type=F32)
    if final:
        ms = jnp.mean(xn * xn, axis=-1, keepdims=True)
        xn = xn * lax.rsqrt(ms + NORM_EPS) * fw_ref[...]
    o_ref[...] = xn


def _outproj(x2, y_a, o_f, o_b, proj, dn_norm_w, w_out_all, final_w, layer, *, final):
    t = x2.shape[0]
    return pl.pallas_call(
        functools.partial(_outproj_kernel, final=final),
        grid=(t // TM_OUT,),
        in_specs=[
            pl.BlockSpec((TM_OUT, D_MODEL), lambda i: (i, 0)),
            pl.BlockSpec((TM_OUT, D_SGU), lambda i: (i, 0)),
            pl.BlockSpec((TM_OUT, D_DN), lambda i: (i, 0)),
            pl.BlockSpec((TM_OUT, D_DN), lambda i: (i, 0)),
            pl.BlockSpec((TM_OUT, COL_BLOCK), lambda i: (i, CB_ZB)),
            pl.BlockSpec((1, DN_HEAD_DIM), lambda i: (0, 0)),
            pl.BlockSpec((1, D_MODEL, D_MODEL), lambda i: (layer, 0, 0)),
            pl.BlockSpec((1, D_MODEL), lambda i: (0, 0)),
        ],
        out_specs=pl.BlockSpec((TM_OUT, D_MODEL), lambda i: (i, 0)),
        out_shape=jax.ShapeDtypeStruct((t, D_MODEL), F32),
        scratch_shapes=[pltpu.VMEM((TM_OUT, D_MODEL), BF16)],
        compiler_params=pltpu.CompilerParams(
            dimension_semantics=("parallel",), vmem_limit_bytes=VMEM_LIMIT),
        name="outproj_final" if final else "outproj",
    )(x2, y_a, o_f, o_b, proj, dn_norm_w, w_out_all, final_w)


def _cast_kernel(w_ref, o_ref):
    o_ref[...] = w_ref[...].astype(BF16)


def _cast_bf16(w):
    nl, rows, cols = w.shape
    return pl.pallas_call(
        _cast_kernel,
        grid=(nl, rows // R_CAST),
        in_specs=[pl.BlockSpec((1, R_CAST, cols), lambda l, i: (l, i, 0))],
        out_specs=pl.BlockSpec((1, R_CAST, cols), lambda l, i: (l, i, 0)),
        out_shape=jax.ShapeDtypeStruct(w.shape, BF16),
        compiler_params=pltpu.CompilerParams(
            dimension_semantics=("parallel", "parallel"), vmem_limit_bytes=VMEM_LIMIT),
        name="cast_bf16",
    )(w)


def _pad_lanes(row):
    return jnp.pad(row, (0, LANES - row.shape[0]))[None, :]


def kernel(x, norm_w, w_in, sgu_ln_g, sgu_ln_b, sgu_w, sgu_b, conv_w, a_log_f, a_log_b,
           dt_bias_f, dt_bias_b, dn_norm_w, w_out, final_norm_w):
    b, s, _ = x.shape
    t = b * s
    depth = w_in.shape[0]
    x2 = x.reshape(t, D_MODEL)
    w_all = _cast_bf16(w_in)
    w_gate = jnp.pad(w_in[:, :, D_MAIN:], ((0, 0), (0, 0), (0, LANES - N_GATE)))
    wg_hi = w_gate.astype(BF16)
    wg_lo = (w_gate - wg_hi.astype(F32)).astype(BF16)
    wg_all = jnp.concatenate([wg_hi, wg_lo], axis=-1)
    ws_all = sgu_w.astype(BF16)
    w_out_all = _cast_bf16(w_out)
    for l in range(depth):
        proj, gates_raw = _inproj(x2, norm_w[l][None, :], w_all, wg_all, l)
        alog_row = _pad_lanes(jnp.concatenate([a_log_f[l], a_log_b[l]]))
        dtb_row = _pad_lanes(jnp.concatenate([dt_bias_f[l], dt_bias_b[l]]))
        gc, gt = _gates(gates_raw, alog_row, dtb_row)
        y_a = _sgu(proj, sgu_ln_g[l][None, :], sgu_ln_b[l][None, :], ws_all, sgu_b[l].T, l)
        proj3 = proj.reshape(b, s, D_MAIN)
        qkn = _prep(proj3, conv_w[l], first_block=CB_Q, n_blocks=2, normalize=True)
        vc = _prep(proj3, conv_w[l], first_block=CB_Q + 2, n_blocks=1, normalize=False)
        gc3 = gc.reshape(b, s, LANES)
        gt4 = gt.reshape(b, s // DN_CHUNK, N_GATE, DN_CHUNK)
        o_f, o_b = _delta(qkn, vc, gc3, gt4)
        x2 = _outproj(x2, y_a, o_f.reshape(t, D_DN), o_b.reshape(t, D_DN), proj,
                      dn_norm_w[l][None, :], w_out_all, final_norm_w[None, :], l,
                      final=(l == depth - 1))
    return x2.reshape(b, s, D_MODEL)
```

```python
import functools

import jax
import jax.numpy as jnp
from jax import lax
from jax.experimental import pallas as pl
from jax.experimental.pallas import tpu as pltpu

D_MODEL = 2048
D_SGU = 1024
SGU_GROUPS = 8
SGU_GROUP_DIM = 128
SGU_CHUNK = 128
D_DN = 1024
DN_HEADS = 8
DN_HEAD_DIM = 128
DN_CHUNK = 128
CONV_WIDTH = 5
CONV_PAD = CONV_WIDTH // 2
NORM_EPS = 1e-6
D_MAIN = 3 * D_SGU + 4 * D_DN
N_GATE = 4 * DN_HEADS
LANES = 128
SUBLANES = 8
BF16_ROWS = 16
COL_BLOCK = 1024

CB_U, CB_V, CB_ZA, CB_Q, CB_ZB = 0, 1, 2, 3, 6
GC_FWD, GC_BWD, BETA_FWD, BETA_BWD = 0, DN_HEADS, 2 * DN_HEADS, 3 * DN_HEADS

F32 = jnp.float32
BF16 = jnp.bfloat16
HIGHEST = lax.Precision.HIGHEST

TM_IN = 1024
TN_IN = 1792
R_GATE = 1024
R_SGU = 512
R_PREP = 512
PREP_TILE = 128
TM_OUT = 512
DELTA_NCH = 2
W_IN_CAST_ROWS = 800
W_OUT_CAST_ROWS = 512
VMEM_LIMIT = 52 * 1024 * 1024


def _sigmoid(x):
    return 1.0 / (1.0 + jnp.exp(-x))


def _gelu(x):
    return 0.5 * x * (1.0 + lax.erf(x * (2.0 ** -0.5)))


def _mm(a, b):
    return jnp.dot(a.astype(BF16), b.astype(BF16), preferred_element_type=F32)


def _nt_dot(a, b):
    return lax.dot_general(a, b, (((1,), (1,)), ((), ())), preferred_element_type=F32)


def _tn_dot(a, b):
    return lax.dot_general(a, b, (((0,), (0,)), ((), ())), preferred_element_type=F32)


def _inproj_kernel(x_ref, nw_ref, w_ref, wg_ref, out_ref, gate_ref, hb_ref):
    @pl.when(pl.program_id(1) == 0)
    def _():
        x = x_ref[...]
        ms = jnp.mean(x * x, axis=-1, keepdims=True)
        h = x * lax.rsqrt(ms + NORM_EPS) * nw_ref[...]
        hb = h.astype(BF16)
        hb_ref[...] = hb
        h_lo = (h - hb.astype(F32)).astype(BF16)
        g_hi = _nt_dot(hb, wg_ref[0])
        g_lo = _nt_dot(h_lo, wg_ref[0, 0:LANES, :])
        gate_ref[...] = g_hi[:, 0:LANES] + g_hi[:, LANES:2 * LANES] + g_lo

    out_ref[...] = _nt_dot(hb_ref[...], w_ref[0]).astype(BF16)


def _inproj(x2, norm_w, w_all, wg_all, layer):
    t = x2.shape[0]
    return pl.pallas_call(
        _inproj_kernel,
        grid=(t // TM_IN, D_MAIN // TN_IN),
        in_specs=[
            pl.BlockSpec((TM_IN, D_MODEL), lambda i, j: (i, 0)),
            pl.BlockSpec((1, D_MODEL), lambda i, j: (0, 0)),
            pl.BlockSpec((1, TN_IN, D_MODEL), lambda i, j: (layer, j, 0)),
            pl.BlockSpec((1, 2 * LANES, D_MODEL), lambda i, j: (layer, 0, 0)),
        ],
        out_specs=[
            pl.BlockSpec((TM_IN, TN_IN), lambda i, j: (i, j)),
            pl.BlockSpec((TM_IN, LANES), lambda i, j: (i, 0)),
        ],
        out_shape=[
            jax.ShapeDtypeStruct((t, D_MAIN), BF16),
            jax.ShapeDtypeStruct((t, LANES), F32),
        ],
        scratch_shapes=[pltpu.VMEM((TM_IN, D_MODEL), BF16)],
        compiler_params=pltpu.CompilerParams(
            dimension_semantics=("parallel", "arbitrary"),
            vmem_limit_bytes=VMEM_LIMIT),
        name="inproj",
    )(x2, norm_w, w_all, wg_all)


def _gate_kernel(g_ref, alog_ref, dtb_ref, gc_ref, gt_ref):
    r = g_ref.shape[0]
    c = DN_CHUNK
    lane = lax.broadcasted_iota(jnp.int32, (c, LANES), 1)
    ri = lax.broadcasted_iota(jnp.int32, (c, c), 0)
    ci = lax.broadcasted_iota(jnp.int32, (c, c), 1)
    lower = jnp.where(ri >= ci, 1.0, 0.0).astype(F32)
    upper = jnp.where(ri <= ci, 1.0, 0.0).astype(F32)
    neg_a = -jnp.exp(alog_ref[...])
    for cc in range(r // c):
        rows = slice(cc * c, (cc + 1) * c)
        raw = g_ref[rows, :]
        sp_in = raw + dtb_ref[...]
        softplus = jnp.maximum(sp_in, 0.0) + jnp.log1p(jnp.exp(-jnp.abs(sp_in)))
        g = jnp.where(lane < BETA_FWD, neg_a * softplus, 0.0)
        beta = _sigmoid(raw)
        pre = jnp.dot(lower, g, precision=HIGHEST, preferred_element_type=F32)
        suf = jnp.dot(upper, g, precision=HIGHEST, preferred_element_type=F32)
        out = jnp.where(lane < GC_BWD, pre,
                        jnp.where(lane < BETA_FWD, suf,
                                  jnp.where(lane < N_GATE, beta, 0.0)))
        gc_ref[rows, :] = out
        gt_ref[cc] = out.T[0:N_GATE, :]


def _gates(gates_raw, alog_row, dtb_row):
    t = gates_raw.shape[0]
    return pl.pallas_call(
        _gate_kernel,
        grid=(t // R_GATE,),
        in_specs=[
            pl.BlockSpec((R_GATE, LANES), lambda i: (i, 0)),
            pl.BlockSpec((1, LANES), lambda i: (0, 0)),
            pl.BlockSpec((1, LANES), lambda i: (0, 0)),
        ],
        out_specs=[
            pl.BlockSpec((R_GATE, LANES), lambda i: (i, 0)),
            pl.BlockSpec((R_GATE // DN_CHUNK, N_GATE, DN_CHUNK), lambda i: (i, 0, 0)),
        ],
        out_shape=[
            jax.ShapeDtypeStruct((t, LANES), F32),
            jax.ShapeDtypeStruct((t // DN_CHUNK, N_GATE, DN_CHUNK), F32),
        ],
        compiler_params=pltpu.CompilerParams(dimension_semantics=("parallel",)),
        name="gates",
    )(gates_raw, alog_row, dtb_row)


def _sgu_kernel(u_ref, v_ref, z_ref, lg_ref, lb_ref, ws_ref, bs_ref, o_ref, vn_ref):
    r = u_ref.shape[0]
    v = _gelu(v_ref[...].astype(F32))
    mu = jnp.mean(v, axis=-1, keepdims=True)
    vc = v - mu
    var = jnp.mean(vc * vc, axis=-1, keepdims=True)
    vn = vc * lax.rsqrt(var + NORM_EPS) * lg_ref[...] + lb_ref[...]
    vn_ref[...] = vn.astype(BF16)
    for c in range(r // SGU_CHUNK):
        rows = slice(c * SGU_CHUNK, (c + 1) * SGU_CHUNK)
        for g in range(SGU_GROUPS):
            cols = slice(g * SGU_GROUP_DIM, (g + 1) * SGU_GROUP_DIM)
            sp = jnp.dot(ws_ref[0, g], vn_ref[rows, cols], preferred_element_type=F32)
            sp = sp + bs_ref[:, g:g + 1]
            u = _gelu(u_ref[rows, cols].astype(F32))
            z = z_ref[rows, cols].astype(F32)
            o_ref[rows, cols] = (u * sp * (z * _sigmoid(z))).astype(BF16)


def _sgu(proj, ln_g, ln_b, ws_all, bs_t, layer):
    t = proj.shape[0]
    return pl.pallas_call(
        _sgu_kernel,
        grid=(t // R_SGU,),
        in_specs=[
            pl.BlockSpec((R_SGU, COL_BLOCK), lambda i: (i, CB_U)),
            pl.BlockSpec((R_SGU, COL_BLOCK), lambda i: (i, CB_V)),
            pl.BlockSpec((R_SGU, COL_BLOCK), lambda i: (i, CB_ZA)),
            pl.BlockSpec((1, D_SGU), lambda i: (0, 0)),
            pl.BlockSpec((1, D_SGU), lambda i: (0, 0)),
            pl.BlockSpec((1, SGU_GROUPS, SGU_CHUNK, SGU_CHUNK), lambda i: (layer, 0, 0, 0)),
            pl.BlockSpec((SGU_CHUNK, SGU_GROUPS), lambda i: (0, 0)),
        ],
        out_specs=pl.BlockSpec((R_SGU, D_SGU), lambda i: (i, 0)),
        out_shape=jax.ShapeDtypeStruct((t, D_SGU), BF16),
        scratch_shapes=[pltpu.VMEM((R_SGU, D_SGU), BF16)],
        compiler_params=pltpu.CompilerParams(
            dimension_semantics=("parallel",), vmem_limit_bytes=VMEM_LIMIT),
        name="sgu",
    )(proj, proj, proj, ln_g, ln_b, ws_all, bs_t)


def _prep_kernel(x_ref, p_ref, n_ref, cw_ref, o_ref, pad_ref, *, normalize):
    kind = pl.program_id(0)
    i = pl.program_id(2)
    r = x_ref.shape[1]
    halo = BF16_ROWS
    zeros = jnp.zeros((halo, COL_BLOCK), BF16)
    pad_ref[0:halo, :] = jnp.where(i > 0, p_ref[0], zeros)
    pad_ref[halo:halo + r, :] = x_ref[0]
    pad_ref[halo + r:2 * halo + r, :] = jnp.where(i < pl.num_programs(2) - 1, n_ref[0], zeros)
    tile = PREP_TILE
    kdim = tile + 2 * halo
    ri = lax.broadcasted_iota(jnp.int32, (tile, kdim), 0)
    ci = lax.broadcasted_iota(jnp.int32, (tile, kdim), 1)
    taps = [j for j in range(CONV_WIDTH) if j != CONV_PAD]
    shift = jnp.concatenate(
        [jnp.where(ci == ri + halo + (j - CONV_PAD), 1.0, 0.0).astype(BF16) for j in taps],
        axis=0)
    scale = jnp.where(kind == 0, DN_HEAD_DIM ** -0.5, 1.0).astype(F32)
    for a in range(r // tile):
        rows = slice(a * tile, (a + 1) * tile)
        xs = pad_ref[a * tile:a * tile + kdim, :]
        shifted = jnp.dot(shift, xs, preferred_element_type=F32)
        acc = cw_ref[CONV_PAD:CONV_PAD + 1, :] * xs[halo:halo + tile, :].astype(F32)
        for n, j in enumerate(taps):
            acc = acc + cw_ref[j:j + 1, :] * shifted[n * tile:(n + 1) * tile, :]
        y = acc * _sigmoid(acc)
        if not normalize:
            o_ref[0, 0, rows, :] = y
            continue
        for h in range(DN_HEADS):
            cols = slice(h * DN_HEAD_DIM, (h + 1) * DN_HEAD_DIM)
            yh = y[:, cols]
            ss = jnp.sum(yh * yh, axis=-1, keepdims=True)
            o_ref[0, 0, rows, cols] = yh * (lax.rsqrt(ss + NORM_EPS) * scale)


def _prep(proj3, conv_w, *, first_block, n_blocks, normalize):
    b, s, _ = proj3.shape
    nhalo = s // BF16_ROWS
    rb = R_PREP // BF16_ROWS
    cw0 = first_block - CB_Q
    return pl.pallas_call(
        functools.partial(_prep_kernel, normalize=normalize),
        grid=(n_blocks, b, s // R_PREP),
        in_specs=[
            pl.BlockSpec((1, R_PREP, COL_BLOCK), lambda kd, bb, i: (bb, i, first_block + kd)),
            pl.BlockSpec((1, BF16_ROWS, COL_BLOCK),
                         lambda kd, bb, i: (bb, jnp.maximum(i * rb - 1, 0), first_block + kd)),
            pl.BlockSpec((1, BF16_ROWS, COL_BLOCK),
                         lambda kd, bb, i: (bb, jnp.minimum((i + 1) * rb, nhalo - 1),
                                            first_block + kd)),
            pl.BlockSpec((CONV_WIDTH, COL_BLOCK), lambda kd, bb, i: (0, cw0 + kd)),
        ],
        out_specs=pl.BlockSpec((1, 1, R_PREP, D_DN), lambda kd, bb, i: (kd, bb, i, 0)),
        out_shape=jax.ShapeDtypeStruct((n_blocks, b, s, D_DN), F32),
        scratch_shapes=[pltpu.VMEM((R_PREP + 2 * BF16_ROWS, COL_BLOCK), BF16)],
        compiler_params=pltpu.CompilerParams(
            dimension_semantics=("parallel", "parallel", "parallel"),
            vmem_limit_bytes=VMEM_LIMIT),
        name="prep_qk" if normalize else "prep_v",
    )(proj3, proj3, proj3, conv_w)


def _tri_inverse(ms, eye, ri, ci):
    blk = SUBLANES
    same = (ri // blk) == (ci // blk)
    p = [jnp.where(same, m, 0.0).astype(BF16) for m in ms]
    p2 = [_mm(x, x) for x in p]
    p2b = [x.astype(BF16) for x in p2]
    p34 = [_mm(y, jnp.concatenate([x, y], axis=1)) for x, y in zip(p, p2b)]
    d = [eye - x + y - z[:, :DN_CHUNK] for x, y, z in zip(p, p2, p34)]
    d = [x + _mm(z[:, DN_CHUNK:], x) for x, z in zip(d, p34)]
    while blk < DN_CHUNK:
        same2 = (ri // (2 * blk)) == (ci // (2 * blk))
        sel = same2 & jnp.logical_not(same)
        off = [jnp.where(sel, m, 0.0).astype(BF16) for m in ms]
        db = [x.astype(BF16) for x in d]
        t = [_mm(x, y) for x, y in zip(db, off)]
        d = [x - _mm(y, z) for x, y, z in zip(d, t, db)]
        same = same2
        blk *= 2
    return d


def _delta_kernel(qf_ref, kf_ref, vf_ref, gcf_ref, gtf_ref,
                  qb_ref, kb_ref, vb_ref, gcb_ref, gtb_ref,
                  of_ref, ob_ref, s_ref):
    @pl.when(pl.program_id(1) == 0)
    def _():
        s_ref[...] = jnp.zeros_like(s_ref)

    c = DN_CHUNK
    ri = lax.broadcasted_iota(jnp.int32, (c, c), 0)
    ci = lax.broadcasted_iota(jnp.int32, (c, c), 1)
    eye = jnp.where(ri == ci, 1.0, 0.0).astype(F32)
    fwd = dict(q=qf_ref, k=kf_ref, v=vf_ref, gc=gcf_ref, gt=gtf_ref, o=of_ref,
               incl=ri >= ci, strict=ri > ci, g_off=GC_FWD, b_off=BETA_FWD, last=c - 1,
               s0=0, order=list(range(DELTA_NCH)))
    bwd = dict(q=qb_ref, k=kb_ref, v=vb_ref, gc=gcb_ref, gt=gtb_ref, o=ob_ref,
               incl=ri <= ci, strict=ri < ci, g_off=GC_BWD, b_off=BETA_BWD, last=0,
               s0=DN_HEADS, order=list(range(DELTA_NCH - 1, -1, -1)))
    probs = [(d, t, h) for t in range(DELTA_NCH) for d in (fwd, bwd) for h in range(DN_HEADS)]
    n = range(len(probs))
    rows = [slice(d["order"][t] * c, (d["order"][t] + 1) * c) for d, t, h in probs]
    cols = [slice(h * DN_HEAD_DIM, (h + 1) * DN_HEAD_DIM) for d, t, h in probs]
    q = [d["q"][0, 0, rows[i], cols[i]] for i, (d, t, h) in enumerate(probs)]
    k = [d["k"][0, 0, rows[i], cols[i]] for i, (d, t, h) in enumerate(probs)]
    v = [d["v"][0, 0, rows[i], cols[i]] for i, (d, t, h) in enumerate(probs)]
    gcol = [d["gc"][0, rows[i], d["g_off"] + h:d["g_off"] + h + 1]
            for i, (d, t, h) in enumerate(probs)]
    bcol = [d["gc"][0, rows[i], d["b_off"] + h:d["b_off"] + h + 1]
            for i, (d, t, h) in enumerate(probs)]
    grow = [d["gt"][0, d["order"][t], d["g_off"] + h:d["g_off"] + h + 1, :]
            for d, t, h in probs]
    glast = [grow[i][:, d["last"]:d["last"] + 1] for i, (d, t, h) in enumerate(probs)]
    kbeta = [k[i] * bcol[i] for i in n]
    kb = [k[i].astype(BF16) for i in n]
    kq = [_nt_dot(jnp.concatenate([kbeta[i], q[i]], axis=0).astype(BF16), kb[i])
          for i in n]
    kk = [kq[i][:c] for i in n]
    qk = [kq[i][c:] for i in n]
    decay = [jnp.where(d["incl"], jnp.exp(gcol[i] - grow[i]), 0.0)
             for i, (d, t, h) in enumerate(probs)]
    m = [jnp.where(d["strict"], kk[i] * decay[i], 0.0) for i, (d, t, h) in enumerate(probs)]
    t_inv = _tri_inverse(m, eye, ri, ci)
    eg = [jnp.exp(gcol[i]) for i in n]
    rhs = [jnp.concatenate([v[i] * bcol[i], kbeta[i] * eg[i]], axis=1) for i in n]
    uw = [_mm(t_inv[i], rhs[i]) for i in n]
    attn = [(qk[i] * decay[i]).astype(BF16) for i in n]
    lhs = [jnp.concatenate([uw[i][:, DN_HEAD_DIM:], q[i] * eg[i]], axis=0).astype(BF16)
           for i in n]
    k_dec = [(k[i] * jnp.exp(glast[i] - gcol[i])).astype(BF16) for i in n]
    g_chunk = [jnp.exp(glast[i]) for i in n]
    state = {(d["s0"], h): s_ref[d["s0"] + h] for d in (fwd, bwd) for h in range(DN_HEADS)}
    for step in range(DELTA_NCH):
        cur = [i for i, (d, t, h) in enumerate(probs) if t == step]
        key = {i: (probs[i][0]["s0"], probs[i][2]) for i in cur}
        ws = {i: _mm(lhs[i], state[key[i]]) for i in cur}
        v_new = {i: (uw[i][:, :DN_HEAD_DIM] - ws[i][:c]).astype(BF16) for i in cur}
        o = {i: ws[i][c:] + jnp.dot(attn[i], v_new[i], preferred_element_type=F32)
             for i in cur}
        s_new = {i: state[key[i]] * g_chunk[i] + _tn_dot(k_dec[i], v_new[i]) for i in cur}
        for i in cur:
            state[key[i]] = s_new[i]
            probs[i][0]["o"][0, rows[i], cols[i]] = o[i].astype(BF16)
    for (s0, h), val in state.items():
        s_ref[s0 + h] = val


def _delta(qkn, vc, gc3, gt4):
    _, b, s, _ = qkn.shape
    r = DELTA_NCH * DN_CHUNK
    nb = s // r

    def dir_specs(bidx):
        return [
            pl.BlockSpec((1, 1, r, D_DN), lambda bb, i: (0, bb, bidx(i), 0)),
            pl.BlockSpec((1, 1, r, D_DN), lambda bb, i: (1, bb, bidx(i), 0)),
            pl.BlockSpec((1, 1, r, D_DN), lambda bb, i: (0, bb, bidx(i), 0)),
            pl.BlockSpec((1, r, LANES), lambda bb, i: (bb, bidx(i), 0)),
            pl.BlockSpec((1, DELTA_NCH, N_GATE, DN_CHUNK), lambda bb, i: (bb, bidx(i), 0, 0)),
        ]

    up = lambda i: i
    down = lambda i: nb - 1 - i
    return pl.pallas_call(
        _delta_kernel,
        grid=(b, nb),
        in_specs=dir_specs(up) + dir_specs(down),
        out_specs=[
            pl.BlockSpec((1, r, D_DN), lambda bb, i: (bb, up(i), 0)),
            pl.BlockSpec((1, r, D_DN), lambda bb, i: (bb, down(i), 0)),
        ],
        out_shape=[jax.ShapeDtypeStruct((b, s, D_DN), BF16)] * 2,
        scratch_shapes=[pltpu.VMEM((2 * DN_HEADS, DN_HEAD_DIM, DN_HEAD_DIM), F32)],
        compiler_params=pltpu.CompilerParams(
            dimension_semantics=("parallel", "arbitrary"),
            vmem_limit_bytes=VMEM_LIMIT),
        name="delta",
    )(qkn, qkn, vc, gc3, gt4, qkn, qkn, vc, gc3, gt4)


def _outproj_kernel(x_ref, ya_ref, of_ref, ob_ref, zb_ref, nw_ref, w_ref, fw_ref,
                    o_ref, y_ref, *, final):
    y_ref[:, 0:D_SGU] = ya_ref[...]
    for h in range(DN_HEADS):
        cols = slice(h * DN_HEAD_DIM, (h + 1) * DN_HEAD_DIM)
        o = of_ref[:, cols].astype(F32) + ob_ref[:, cols].astype(F32)
        ms = jnp.mean(o * o, axis=-1, keepdims=True)
        z = zb_ref[:, cols].astype(F32)
        yh = o * lax.rsqrt(ms + NORM_EPS) * nw_ref[...] * (z * _sigmoid(z))
        y_ref[:, D_SGU + h * DN_HEAD_DIM:D_SGU + (h + 1) * DN_HEAD_DIM] = yh.astype(BF16)
    xn = x_ref[...] + jnp.dot(y_ref[...], w_ref[0], preferred_element_type=F32)
    if final:
        ms = jnp.mean(xn * xn, axis=-1, keepdims=True)
        xn = xn * lax.rsqrt(ms + NORM_EPS) * fw_ref[...]
    o_ref[...] = xn


def _outproj(x2, y_a, o_f, o_b, proj, dn_norm_w, w_out_all, final_w, layer, *, final):
    t = x2.shape[0]
    return pl.pallas_call(
        functools.partial(_outproj_kernel, final=final),
        grid=(t // TM_OUT,),
        in_specs=[
            pl.BlockSpec((TM_OUT, D_MODEL), lambda i: (i, 0)),
            pl.BlockSpec((TM_OUT, D_SGU), lambda i: (i, 0)),
            pl.BlockSpec((TM_OUT, D_DN), lambda i: (i, 0)),
            pl.BlockSpec((TM_OUT, D_DN), lambda i: (i, 0)),
            pl.BlockSpec((TM_OUT, COL_BLOCK), lambda i: (i, CB_ZB)),
            pl.BlockSpec((1, DN_HEAD_DIM), lambda i: (0, 0)),
            pl.BlockSpec((1, D_MODEL, D_MODEL), lambda i: (layer, 0, 0)),
            pl.BlockSpec((1, D_MODEL), lambda i: (0, 0)),
        ],
        out_specs=pl.BlockSpec((TM_OUT, D_MODEL), lambda i: (i, 0)),
        out_shape=jax.ShapeDtypeStruct((t, D_MODEL), F32),
        scratch_shapes=[pltpu.VMEM((TM_OUT, D_MODEL), BF16)],
        compiler_params=pltpu.CompilerParams(
            dimension_semantics=("parallel",), vmem_limit_bytes=VMEM_LIMIT),
        name="outproj_final" if final else "outproj",
    )(x2, y_a, o_f, o_b, proj, dn_norm_w, w_out_all, final_w)


def _cast_kernel(w_ref, o_ref):
    o_ref[...] = w_ref[...].astype(BF16)


def _cast_bf16(w, rows_per_step):
    nl, rows, cols = w.shape
    assert rows % rows_per_step == 0 and rows_per_step % BF16_ROWS == 0
    return pl.pallas_call(
        _cast_kernel,
        grid=(nl, rows // rows_per_step),
        in_specs=[pl.BlockSpec((1, rows_per_step, cols), lambda l, i: (l, i, 0))],
        out_specs=pl.BlockSpec((1, rows_per_step, cols), lambda l, i: (l, i, 0)),
        out_shape=jax.ShapeDtypeStruct(w.shape, BF16),
        compiler_params=pltpu.CompilerParams(
            dimension_semantics=("parallel", "parallel"), vmem_limit_bytes=VMEM_LIMIT),
        name="cast_bf16",
    )(w)


def _pad_lanes(row):
    return jnp.pad(row, (0, LANES - row.shape[0]))[None, :]


def kernel(x, norm_w, w_in, sgu_ln_g, sgu_ln_b, sgu_w, sgu_b, conv_w, a_log_f, a_log_b,
           dt_bias_f, dt_bias_b, dn_norm_w, w_out, final_norm_w):
    b, s, _ = x.shape
    t = b * s
    depth = w_in.shape[0]
    x2 = x.reshape(t, D_MODEL)
    w_t = jnp.swapaxes(w_in, 1, 2)
    w_all = _cast_bf16(w_t, W_IN_CAST_ROWS)
    w_gate = jnp.pad(w_t[:, D_MAIN:, :], ((0, 0), (0, LANES - N_GATE), (0, 0)))
    wg_hi = w_gate.astype(BF16)
    wg_lo = (w_gate - wg_hi.astype(F32)).astype(BF16)
    wg_all = jnp.concatenate([wg_hi, wg_lo], axis=1)
    ws_all = sgu_w.astype(BF16)
    w_out_all = _cast_bf16(w_out, W_OUT_CAST_ROWS)
    for l in range(depth):
        proj, gates_raw = _inproj(x2, norm_w[l][None, :], w_all, wg_all, l)
        alog_row = _pad_lanes(jnp.concatenate([a_log_f[l], a_log_b[l]]))
        dtb_row = _pad_lanes(jnp.concatenate([dt_bias_f[l], dt_bias_b[l]]))
        gc, gt = _gates(gates_raw, alog_row, dtb_row)
        y_a = _sgu(proj, sgu_ln_g[l][None, :], sgu_ln_b[l][None, :], ws_all, sgu_b[l].T, l)
        proj3 = proj.reshape(b, s, D_MAIN)
        qkn = _prep(proj3, conv_w[l], first_block=CB_Q, n_blocks=2, normalize=True)
        vc = _prep(proj3, conv_w[l], first_block=CB_Q + 2, n_blocks=1, normalize=False)
        gc3 = gc.reshape(b, s, LANES)
        gt4 = gt.reshape(b, s // DN_CHUNK, N_GATE, DN_CHUNK)
        o_f, o_b = _delta(qkn, vc, gc3, gt4)
        x2 = _outproj(x2, y_a, o_f.reshape(t, D_DN), o_b.reshape(t, D_DN), proj,
                      dn_norm_w[l][None, :], w_out_all, final_norm_w[None, :], l,
                      final=(l == depth - 1))
    return x2.reshape(b, s, D_MODEL)
```

```python
import functools

import jax
import jax.numpy as jnp
from jax import lax
from jax.experimental import pallas as pl
from jax.experimental.pallas import tpu as pltpu

D_MODEL = 2048
D_SGU = 1024
SGU_GROUPS = 8
SGU_GROUP_DIM = 128
SGU_CHUNK = 128
D_DN = 1024
DN_HEADS = 8
DN_HEAD_DIM = 128
DN_CHUNK = 128
CONV_WIDTH = 5
CONV_PAD = CONV_WIDTH // 2
NORM_EPS = 1e-6
D_MAIN = 3 * D_SGU + 4 * D_DN
N_GATE = 4 * DN_HEADS
LANES = 128
SUBLANES = 8
BF16_ROWS = 16
COL_BLOCK = 1024

CB_U, CB_V, CB_ZA, CB_Q, CB_ZB = 0, 1, 2, 3, 6
GC_FWD, GC_BWD, BETA_FWD, BETA_BWD = 0, DN_HEADS, 2 * DN_HEADS, 3 * DN_HEADS

F32 = jnp.float32
BF16 = jnp.bfloat16
HIGHEST = lax.Precision.HIGHEST

TM_IN = 1024
TN_IN = 1792
R_GATE = 1024
R_SGU = 512
R_PREP = 512
PREP_TILE = 128
TM_OUT = 512
DELTA_NCH = 4
SOLVE_BLOCK = 32
DELTA_SKEW = 6
W_IN_CAST_ROWS = 800
W_OUT_CAST_ROWS = 512
VMEM_LIMIT = 52 * 1024 * 1024


def _sigmoid(x):
    return 1.0 / (1.0 + jnp.exp(-x))


def _gelu(x):
    return 0.5 * x * (1.0 + lax.erf(x * (2.0 ** -0.5)))


def _mm(a, b):
    return jnp.dot(a.astype(BF16), b.astype(BF16), preferred_element_type=F32)


def _nt_dot(a, b):
    return lax.dot_general(a, b, (((1,), (1,)), ((), ())), preferred_element_type=F32)


def _tn_dot(a, b):
    return lax.dot_general(a, b, (((0,), (0,)), ((), ())), preferred_element_type=F32)


def _inproj_kernel(x_ref, nw_ref, w_ref, wg_ref, out_ref, gate_ref, hb_ref):
    @pl.when(pl.program_id(1) == 0)
    def _():
        x = x_ref[...]
        ms = jnp.mean(x * x, axis=-1, keepdims=True)
        h = x * lax.rsqrt(ms + NORM_EPS) * nw_ref[...]
        hb = h.astype(BF16)
        hb_ref[...] = hb
        h_lo = (h - hb.astype(F32)).astype(BF16)
        g_hi = _nt_dot(hb, wg_ref[0])
        g_lo = _nt_dot(h_lo, wg_ref[0, 0:LANES, :])
        gate_ref[...] = g_hi[:, 0:LANES] + g_hi[:, LANES:2 * LANES] + g_lo

    out_ref[...] = _nt_dot(hb_ref[...], w_ref[0]).astype(BF16)


def _inproj(x2, norm_w, w_all, wg_all, layer):
    t = x2.shape[0]
    return pl.pallas_call(
        _inproj_kernel,
        grid=(t // TM_IN, D_MAIN // TN_IN),
        in_specs=[
            pl.BlockSpec((TM_IN, D_MODEL), lambda i, j: (i, 0)),
            pl.BlockSpec((1, D_MODEL), lambda i, j: (0, 0)),
            pl.BlockSpec((1, TN_IN, D_MODEL), lambda i, j: (layer, j, 0)),
            pl.BlockSpec((1, 2 * LANES, D_MODEL), lambda i, j: (layer, 0, 0)),
        ],
        out_specs=[
            pl.BlockSpec((TM_IN, TN_IN), lambda i, j: (i, j)),
            pl.BlockSpec((TM_IN, LANES), lambda i, j: (i, 0)),
        ],
        out_shape=[
            jax.ShapeDtypeStruct((t, D_MAIN), BF16),
            jax.ShapeDtypeStruct((t, LANES), F32),
        ],
        scratch_shapes=[pltpu.VMEM((TM_IN, D_MODEL), BF16)],
        compiler_params=pltpu.CompilerParams(
            dimension_semantics=("parallel", "arbitrary"),
            vmem_limit_bytes=VMEM_LIMIT),
        name="inproj",
    )(x2, norm_w, w_all, wg_all)


def _gate_kernel(g_ref, alog_ref, dtb_ref, gc_ref, gt_ref):
    r = g_ref.shape[0]
    c = DN_CHUNK
    lane = lax.broadcasted_iota(jnp.int32, (c, LANES), 1)
    ri = lax.broadcasted_iota(jnp.int32, (c, c), 0)
    ci = lax.broadcasted_iota(jnp.int32, (c, c), 1)
    lower = jnp.where(ri >= ci, 1.0, 0.0).astype(F32)
    upper = jnp.where(ri <= ci, 1.0, 0.0).astype(F32)
    neg_a = -jnp.exp(alog_ref[...])
    for cc in range(r // c):
        rows = slice(cc * c, (cc + 1) * c)
        raw = g_ref[rows, :]
        sp_in = raw + dtb_ref[...]
        softplus = jnp.maximum(sp_in, 0.0) + jnp.log1p(jnp.exp(-jnp.abs(sp_in)))
        g = jnp.where(lane < BETA_FWD, neg_a * softplus, 0.0)
        beta = _sigmoid(raw)
        pre = jnp.dot(lower, g, precision=HIGHEST, preferred_element_type=F32)
        suf = jnp.dot(upper, g, precision=HIGHEST, preferred_element_type=F32)
        out = jnp.where(lane < GC_BWD, pre,
                        jnp.where(lane < BETA_FWD, suf,
                                  jnp.where(lane < N_GATE, beta, 0.0)))
        gc_ref[rows, :] = out
        gt_ref[cc] = out.T[0:N_GATE, :]


def _gates(gates_raw, alog_row, dtb_row):
    t = gates_raw.shape[0]
    return pl.pallas_call(
        _gate_kernel,
        grid=(t // R_GATE,),
        in_specs=[
            pl.BlockSpec((R_GATE, LANES), lambda i: (i, 0)),
            pl.BlockSpec((1, LANES), lambda i: (0, 0)),
            pl.BlockSpec((1, LANES), lambda i: (0, 0)),
        ],
        out_specs=[
            pl.BlockSpec((R_GATE, LANES), lambda i: (i, 0)),
            pl.BlockSpec((R_GATE // DN_CHUNK, N_GATE, DN_CHUNK), lambda i: (i, 0, 0)),
        ],
        out_shape=[
            jax.ShapeDtypeStruct((t, LANES), F32),
            jax.ShapeDtypeStruct((t // DN_CHUNK, N_GATE, DN_CHUNK), F32),
        ],
        compiler_params=pltpu.CompilerParams(dimension_semantics=("parallel",)),
        name="gates",
    )(gates_raw, alog_row, dtb_row)


def _sgu_kernel(u_ref, v_ref, z_ref, lg_ref, lb_ref, ws_ref, bs_ref, o_ref, vn_ref):
    r = u_ref.shape[0]
    v = _gelu(v_ref[...].astype(F32))
    mu = jnp.mean(v, axis=-1, keepdims=True)
    vc = v - mu
    var = jnp.mean(vc * vc, axis=-1, keepdims=True)
    vn = vc * lax.rsqrt(var + NORM_EPS) * lg_ref[...] + lb_ref[...]
    vn_ref[...] = vn.astype(BF16)
    for c in range(r // SGU_CHUNK):
        rows = slice(c * SGU_CHUNK, (c + 1) * SGU_CHUNK)
        for g in range(SGU_GROUPS):
            cols = slice(g * SGU_GROUP_DIM, (g + 1) * SGU_GROUP_DIM)
            sp = jnp.dot(ws_ref[0, g], vn_ref[rows, cols], preferred_element_type=F32)
            sp = sp + bs_ref[:, g:g + 1]
            u = _gelu(u_ref[rows, cols].astype(F32))
            z = z_ref[rows, cols].astype(F32)
            o_ref[rows, cols] = (u * sp * (z * _sigmoid(z))).astype(BF16)


def _sgu(proj, ln_g, ln_b, ws_all, bs_t, layer):
    t = proj.shape[0]
    return pl.pallas_call(
        _sgu_kernel,
        grid=(t // R_SGU,),
        in_specs=[
            pl.BlockSpec((R_SGU, COL_BLOCK), lambda i: (i, CB_U)),
            pl.BlockSpec((R_SGU, COL_BLOCK), lambda i: (i, CB_V)),
            pl.BlockSpec((R_SGU, COL_BLOCK), lambda i: (i, CB_ZA)),
            pl.BlockSpec((1, D_SGU), lambda i: (0, 0)),
            pl.BlockSpec((1, D_SGU), lambda i: (0, 0)),
            pl.BlockSpec((1, SGU_GROUPS, SGU_CHUNK, SGU_CHUNK), lambda i: (layer, 0, 0, 0)),
            pl.BlockSpec((SGU_CHUNK, SGU_GROUPS), lambda i: (0, 0)),
        ],
        out_specs=pl.BlockSpec((R_SGU, D_SGU), lambda i: (i, 0)),
        out_shape=jax.ShapeDtypeStruct((t, D_SGU), BF16),
        scratch_shapes=[pltpu.VMEM((R_SGU, D_SGU), BF16)],
        compiler_params=pltpu.CompilerParams(
            dimension_semantics=("parallel",), vmem_limit_bytes=VMEM_LIMIT),
        name="sgu",
    )(proj, proj, proj, ln_g, ln_b, ws_all, bs_t)


def _prep_kernel(x_ref, p_ref, n_ref, cw_ref, o_ref, pad_ref, *, normalize):
    kind = pl.program_id(0)
    i = pl.program_id(2)
    r = x_ref.shape[1]
    halo = BF16_ROWS
    zeros = jnp.zeros((halo, COL_BLOCK), BF16)
    pad_ref[0:halo, :] = jnp.where(i > 0, p_ref[0], zeros)
    pad_ref[halo:halo + r, :] = x_ref[0]
    pad_ref[halo + r:2 * halo + r, :] = jnp.where(i < pl.num_programs(2) - 1, n_ref[0], zeros)
    tile = PREP_TILE
    kdim = tile + 2 * halo
    ri = lax.broadcasted_iota(jnp.int32, (tile, kdim), 0)
    ci = lax.broadcasted_iota(jnp.int32, (tile, kdim), 1)
    taps = [j for j in range(CONV_WIDTH) if j != CONV_PAD]
    shift = jnp.concatenate(
        [jnp.where(ci == ri + halo + (j - CONV_PAD), 1.0, 0.0).astype(BF16) for j in taps],
        axis=0)
    scale = jnp.where(kind == 0, DN_HEAD_DIM ** -0.5, 1.0).astype(F32)
    for a in range(r // tile):
        rows = slice(a * tile, (a + 1) * tile)
        xs = pad_ref[a * tile:a * tile + kdim, :]
        shifted = jnp.dot(shift, xs, preferred_element_type=F32)
        acc = cw_ref[CONV_PAD:CONV_PAD + 1, :] * xs[halo:halo + tile, :].astype(F32)
        for n, j in enumerate(taps):
            acc = acc + cw_ref[j:j + 1, :] * shifted[n * tile:(n + 1) * tile, :]
        y = acc * _sigmoid(acc)
        if not normalize:
            o_ref[0, 0, rows, :] = y
            continue
        for h in range(DN_HEADS):
            cols = slice(h * DN_HEAD_DIM, (h + 1) * DN_HEAD_DIM)
            yh = y[:, cols]
            ss = jnp.sum(yh * yh, axis=-1, keepdims=True)
            o_ref[0, 0, rows, cols] = yh * (lax.rsqrt(ss + NORM_EPS) * scale)


def _prep(proj3, conv_w, *, first_block, n_blocks, normalize):
    b, s, _ = proj3.shape
    nhalo = s // BF16_ROWS
    rb = R_PREP // BF16_ROWS
    cw0 = first_block - CB_Q
    return pl.pallas_call(
        functools.partial(_prep_kernel, normalize=normalize),
        grid=(n_blocks, b, s // R_PREP),
        in_specs=[
            pl.BlockSpec((1, R_PREP, COL_BLOCK), lambda kd, bb, i: (bb, i, first_block + kd)),
            pl.BlockSpec((1, BF16_ROWS, COL_BLOCK),
                         lambda kd, bb, i: (bb, jnp.maximum(i * rb - 1, 0), first_block + kd)),
            pl.BlockSpec((1, BF16_ROWS, COL_BLOCK),
                         lambda kd, bb, i: (bb, jnp.minimum((i + 1) * rb, nhalo - 1),
                                            first_block + kd)),
            pl.BlockSpec((CONV_WIDTH, COL_BLOCK), lambda kd, bb, i: (0, cw0 + kd)),
        ],
        out_specs=pl.BlockSpec((1, 1, R_PREP, D_DN), lambda kd, bb, i: (kd, bb, i, 0)),
        out_shape=jax.ShapeDtypeStruct((n_blocks, b, s, D_DN), F32),
        scratch_shapes=[pltpu.VMEM((R_PREP + 2 * BF16_ROWS, COL_BLOCK), BF16)],
        compiler_params=pltpu.CompilerParams(
            dimension_semantics=("parallel", "parallel", "parallel"),
            vmem_limit_bytes=VMEM_LIMIT),
        name="prep_qk" if normalize else "prep_v",
    )(proj3, proj3, proj3, conv_w)


def _tri_solve(ms, rhs_rows, width, upper, eye, ri, ci):
    n = range(len(ms))
    nblk = DN_CHUNK // SOLVE_BLOCK
    mb = [x.astype(BF16) for x in ms]
    d = yield from _tri_inverse(ms, eye, ri, ci, SOLVE_BLOCK)
    db = [x.astype(BF16) for x in d]
    zero = jnp.zeros((SOLVE_BLOCK, width), BF16)
    order = [list(range(nblk - 1, -1, -1)) if upper[i] else list(range(nblk)) for i in n]
    x = [[None] * nblk for _ in n]
    xf = [[None] * nblk for _ in n]

    def place(blocks):
        return jnp.concatenate([zero if b is None else b for b in blocks], axis=0)

    for step in range(nblk):
        rows = [slice(order[i][step] * SOLVE_BLOCK, (order[i][step] + 1) * SOLVE_BLOCK)
                for i in n]
        z = [rhs_rows(i, rows[i]) for i in n]
        if step > 0:
            acc = [jnp.dot(mb[i][rows[i], :], place(x[i]), preferred_element_type=F32)
                   for i in n]
            yield
            z = [z[i] - acc[i] for i in n]
        zfull = []
        for i in n:
            blocks = [None] * nblk
            blocks[order[i][step]] = z[i].astype(BF16)
            zfull.append(place(blocks))
        xr = [jnp.dot(db[i][rows[i], :], zfull[i], preferred_element_type=F32) for i in n]
        yield
        for i in n:
            xf[i][order[i][step]] = xr[i]
            x[i][order[i][step]] = xr[i].astype(BF16)
    return [jnp.concatenate(xf[i], axis=0) for i in n]


def _tri_inverse(ms, eye, ri, ci, size):
    blk = SUBLANES
    same = (ri // blk) == (ci // blk)
    p = [jnp.where(same, m, 0.0).astype(BF16) for m in ms]
    p2 = [_mm(x, x) for x in p]
    yield
    p2b = [x.astype(BF16) for x in p2]
    p34 = [_mm(y, jnp.concatenate([x, y], axis=1)) for x, y in zip(p, p2b)]
    yield
    d = [eye - x + y - z[:, :DN_CHUNK] for x, y, z in zip(p, p2, p34)]
    d = [x + _mm(z[:, DN_CHUNK:], x) for x, z in zip(d, p34)]
    yield
    while blk < size:
        same2 = (ri // (2 * blk)) == (ci // (2 * blk))
        sel = same2 & jnp.logical_not(same)
        off = [jnp.where(sel, m, 0.0).astype(BF16) for m in ms]
        db = [x.astype(BF16) for x in d]
        t = [_mm(x, y) for x, y in zip(db, off)]
        yield
        d = [x - _mm(y, z) for x, y, z in zip(d, t, db)]
        yield
        same = same2
        blk *= 2
    return d


def _delta_kernel(qf_ref, kf_ref, vf_ref, gcf_ref, gtf_ref,
                  qb_ref, kb_ref, vb_ref, gcb_ref, gtb_ref,
                  of_ref, ob_ref, s_ref):
    @pl.when(pl.program_id(1) == 0)
    def _():
        s_ref[...] = jnp.zeros_like(s_ref)

    c = DN_CHUNK
    ri = lax.broadcasted_iota(jnp.int32, (c, c), 0)
    ci = lax.broadcasted_iota(jnp.int32, (c, c), 1)
    eye = jnp.where(ri == ci, 1.0, 0.0).astype(F32)
    fwd = dict(q=qf_ref, k=kf_ref, v=vf_ref, gc=gcf_ref, gt=gtf_ref, o=of_ref,
               incl=ri >= ci, strict=ri > ci, g_off=GC_FWD, b_off=BETA_FWD, last=c - 1,
               s0=0, order=list(range(DELTA_NCH)))
    bwd = dict(q=qb_ref, k=kb_ref, v=vb_ref, gc=gcb_ref, gt=gtb_ref, o=ob_ref,
               incl=ri <= ci, strict=ri < ci, g_off=GC_BWD, b_off=BETA_BWD, last=0,
               s0=DN_HEADS, order=list(range(DELTA_NCH - 1, -1, -1)))
    state = {(d["s0"], h): s_ref[d["s0"] + h] for d in (fwd, bwd) for h in range(DN_HEADS)}
    done = set()
    waves = [_delta_wave([(d, t, h) for d in (fwd, bwd) for h in range(DN_HEADS)],
                         (t, 0), (t - 1, 0), bwd, state, done, eye, ri, ci)
             for t in range(DELTA_NCH)]
    pending, active, tick = list(waves), [], 0
    while pending or active:
        if pending and tick % DELTA_SKEW == 0:
            active.append(pending.pop(0))
        for g in list(active):
            try:
                next(g)
            except StopIteration:
                active.remove(g)
        tick += 1
    for (s0, h), val in state.items():
        s_ref[s0 + h] = val


def _delta_wave(probs, wave, prev_wave, bwd, state, done, eye, ri, ci):
    c = DN_CHUNK
    n = range(len(probs))
    rows = [slice(d["order"][t] * c, (d["order"][t] + 1) * c) for d, t, h in probs]
    cols = [slice(h * DN_HEAD_DIM, (h + 1) * DN_HEAD_DIM) for d, t, h in probs]
    q = [d["q"][0, 0, rows[i], cols[i]] for i, (d, t, h) in enumerate(probs)]
    k = [d["k"][0, 0, rows[i], cols[i]] for i, (d, t, h) in enumerate(probs)]
    v = [d["v"][0, 0, rows[i], cols[i]] for i, (d, t, h) in enumerate(probs)]
    gcol = [d["gc"][0, rows[i], d["g_off"] + h:d["g_off"] + h + 1]
            for i, (d, t, h) in enumerate(probs)]
    bcol = [d["gc"][0, rows[i], d["b_off"] + h:d["b_off"] + h + 1]
            for i, (d, t, h) in enumerate(probs)]
    grow = [d["gt"][0, d["order"][t], d["g_off"] + h:d["g_off"] + h + 1, :]
            for d, t, h in probs]
    glast = [grow[i][:, d["last"]:d["last"] + 1] for i, (d, t, h) in enumerate(probs)]
    kbeta = [k[i] * bcol[i] for i in n]
    kb = [k[i].astype(BF16) for i in n]
    kq_lhs = [jnp.concatenate([kbeta[i], q[i]], axis=0).astype(BF16) for i in n]
    yield
    kq = [_nt_dot(kq_lhs[i], kb[i]) for i in n]
    yield
    decay = [jnp.where(d["incl"], jnp.exp(gcol[i] - grow[i]), 0.0)
             for i, (d, t, h) in enumerate(probs)]
    m = [jnp.where(d["strict"], kq[i][:c] * decay[i], 0.0) for i, (d, t, h) in enumerate(probs)]
    attn = [(kq[i][c:] * decay[i]).astype(BF16) for i in n]
    eg = [jnp.exp(gcol[i]) for i in n]
    rhs = [jnp.concatenate([v[i] * bcol[i], kbeta[i] * eg[i]], axis=1) for i in n]
    upper = [d is bwd for d, t, h in probs]
    uw = yield from _tri_solve(m, lambda i, sub: rhs[i][sub, :], 2 * DN_HEAD_DIM, upper,
                               eye, ri, ci)
    lhs = [jnp.concatenate([uw[i][:, DN_HEAD_DIM:], q[i] * eg[i]], axis=0).astype(BF16)
           for i in n]
    k_dec = [(k[i] * jnp.exp(glast[i] - gcol[i])).astype(BF16) for i in n]
    g_chunk = [jnp.exp(glast[i]) for i in n]
    assert prev_wave[0] < 0 or prev_wave in done
    key = [(d["s0"], h) for d, t, h in probs]
    ws = [_mm(lhs[i], state[key[i]]) for i in n]
    yield
    v_new = [(uw[i][:, :DN_HEAD_DIM] - ws[i][:c]).astype(BF16) for i in n]
    o = [ws[i][c:] + jnp.dot(attn[i], v_new[i], preferred_element_type=F32) for i in n]
    s_new = [state[key[i]] * g_chunk[i] + _tn_dot(k_dec[i], v_new[i]) for i in n]
    yield
    for i, (d, t, h) in enumerate(probs):
        state[key[i]] = s_new[i]
        d["o"][0, rows[i], cols[i]] = o[i].astype(BF16)
    done.add(wave)


def _delta(qkn, vc, gc3, gt4):
    _, b, s, _ = qkn.shape
    r = DELTA_NCH * DN_CHUNK
    nb = s // r

    def dir_specs(bidx):
        return [
            pl.BlockSpec((1, 1, r, D_DN), lambda bb, i: (0, bb, bidx(i), 0)),
            pl.BlockSpec((1, 1, r, D_DN), lambda bb, i: (1, bb, bidx(i), 0)),
            pl.BlockSpec((1, 1, r, D_DN), lambda bb, i: (0, bb, bidx(i), 0)),
            pl.BlockSpec((1, r, LANES), lambda bb, i: (bb, bidx(i), 0)),
            pl.BlockSpec((1, DELTA_NCH, N_GATE, DN_CHUNK), lambda bb, i: (bb, bidx(i), 0, 0)),
        ]

    up = lambda i: i
    down = lambda i: nb - 1 - i
    return pl.pallas_call(
        _delta_kernel,
        grid=(b, nb),
        in_specs=dir_specs(up) + dir_specs(down),
        out_specs=[
            pl.BlockSpec((1, r, D_DN), lambda bb, i: (bb, up(i), 0)),
            pl.BlockSpec((1, r, D_DN), lambda bb, i: (bb, down(i), 0)),
        ],
        out_shape=[jax.ShapeDtypeStruct((b, s, D_DN), BF16)] * 2,
        scratch_shapes=[pltpu.VMEM((2 * DN_HEADS, DN_HEAD_DIM, DN_HEAD_DIM), F32)],
        compiler_params=pltpu.CompilerParams(
            dimension_semantics=("parallel", "arbitrary"),
            vmem_limit_bytes=VMEM_LIMIT),
        name="delta",
    )(qkn, qkn, vc, gc3, gt4, qkn, qkn, vc, gc3, gt4)


def _outproj_kernel(x_ref, ya_ref, of_ref, ob_ref, zb_ref, nw_ref, w_ref, fw_ref,
                    o_ref, y_ref, *, final):
    y_ref[:, 0:D_SGU] = ya_ref[...]
    for h in range(DN_HEADS):
        cols = slice(h * DN_HEAD_DIM, (h + 1) * DN_HEAD_DIM)
        o = of_ref[:, cols].astype(F32) + ob_ref[:, cols].astype(F32)
        ms = jnp.mean(o * o, axis=-1, keepdims=True)
        z = zb_ref[:, cols].astype(F32)
        yh = o * lax.rsqrt(ms + NORM_EPS) * nw_ref[...] * (z * _sigmoid(z))
        y_ref[:, D_SGU + h * DN_HEAD_DIM:D_SGU + (h + 1) * DN_HEAD_DIM] = yh.astype(BF16)
    xn = x_ref[...] + jnp.dot(y_ref[...], w_ref[0], preferred_element_type=F32)
    if final:
        ms = jnp.mean(xn * xn, axis=-1, keepdims=True)
        xn = xn * lax.rsqrt(ms + NORM_EPS) * fw_ref[...]
    o_ref[...] = xn


def _outproj(x2, y_a, o_f, o_b, proj, dn_norm_w, w_out_all, final_w, layer, *, final):
    t = x2.shape[0]
    return pl.pallas_call(
        functools.partial(_outproj_kernel, final=final),
        grid=(t // TM_OUT,),
        in_specs=[
            pl.BlockSpec((TM_OUT, D_MODEL), lambda i: (i, 0)),
            pl.BlockSpec((TM_OUT, D_SGU), lambda i: (i, 0)),
            pl.BlockSpec((TM_OUT, D_DN), lambda i: (i, 0)),
            pl.BlockSpec((TM_OUT, D_DN), lambda i: (i, 0)),
            pl.BlockSpec((TM_OUT, COL_BLOCK), lambda i: (i, CB_ZB)),
            pl.BlockSpec((1, DN_HEAD_DIM), lambda i: (0, 0)),
            pl.BlockSpec((1, D_MODEL, D_MODEL), lambda i: (layer, 0, 0)),
            pl.BlockSpec((1, D_MODEL), lambda i: (0, 0)),
        ],
        out_specs=pl.BlockSpec((TM_OUT, D_MODEL), lambda i: (i, 0)),
        out_shape=jax.ShapeDtypeStruct((t, D_MODEL), F32),
        scratch_shapes=[pltpu.VMEM((TM_OUT, D_MODEL), BF16)],
        compiler_params=pltpu.CompilerParams(
            dimension_semantics=("parallel",), vmem_limit_bytes=VMEM_LIMIT),
        name="outproj_final" if final else "outproj",
    )(x2, y_a, o_f, o_b, proj, dn_norm_w, w_out_all, final_w)


def _cast_kernel(w_ref, o_ref):
    o_ref[...] = w_ref[...].astype(BF16)


def _cast_bf16(w, rows_per_step):
    nl, rows, cols = w.shape
    assert rows % rows_per_step == 0 and rows_per_step % BF16_ROWS == 0
    return pl.pallas_call(
        _cast_kernel,
        grid=(nl, rows // rows_per_step),
        in_specs=[pl.BlockSpec((1, rows_per_step, cols), lambda l, i: (l, i, 0))],
        out_specs=pl.BlockSpec((1, rows_per_step, cols), lambda l, i: (l, i, 0)),
        out_shape=jax.ShapeDtypeStruct(w.shape, BF16),
        compiler_params=pltpu.CompilerParams(
            dimension_semantics=("parallel", "parallel"), vmem_limit_bytes=VMEM_LIMIT),
        name="cast_bf16",
    )(w)


def _pad_lanes(row):
    return jnp.pad(row, (0, LANES - row.shape[0]))[None, :]


def kernel(x, norm_w, w_in, sgu_ln_g, sgu_ln_b, sgu_w, sgu_b, conv_w, a_log_f, a_log_b,
           dt_bias_f, dt_bias_b, dn_norm_w, w_out, final_norm_w):
    b, s, _ = x.shape
    t = b * s
    depth = w_in.shape[0]
    x2 = x.reshape(t, D_MODEL)
    w_t = jnp.swapaxes(w_in, 1, 2)
    w_all = _cast_bf16(w_t, W_IN_CAST_ROWS)
    w_gate = jnp.pad(w_t[:, D_MAIN:, :], ((0, 0), (0, LANES - N_GATE), (0, 0)))
    wg_hi = w_gate.astype(BF16)
    wg_lo = (w_gate - wg_hi.astype(F32)).astype(BF16)
    wg_all = jnp.concatenate([wg_hi, wg_lo], axis=1)
    ws_all = sgu_w.astype(BF16)
    w_out_all = _cast_bf16(w_out, W_OUT_CAST_ROWS)
    for l in range(depth):
        proj, gates_raw = _inproj(x2, norm_w[l][None, :], w_all, wg_all, l)
        alog_row = _pad_lanes(jnp.concatenate([a_log_f[l], a_log_b[l]]))
        dtb_row = _pad_lanes(jnp.concatenate([dt_bias_f[l], dt_bias_b[l]]))
        gc, gt = _gates(gates_raw, alog_row, dtb_row)
        y_a = _sgu(proj, sgu_ln_g[l][None, :], sgu_ln_b[l][None, :], ws_all, sgu_b[l].T, l)
        proj3 = proj.reshape(b, s, D_MAIN)
        qkn = _prep(proj3, conv_w[l], first_block=CB_Q, n_blocks=2, normalize=True)
        vc = _prep(proj3, conv_w[l], first_block=CB_Q + 2, n_blocks=1, normalize=False)
        gc3 = gc.reshape(b, s, LANES)
        gt4 = gt.reshape(b, s // DN_CHUNK, N_GATE, DN_CHUNK)
        o_f, o_b = _delta(qkn, vc, gc3, gt4)
        x2 = _outproj(x2, y_a, o_f.reshape(t, D_DN), o_b.reshape(t, D_DN), proj,
                      dn_norm_w[l][None, :], w_out_all, final_norm_w[None, :], l,
                      final=(l == depth - 1))
    return x2.reshape(b, s, D_MODEL)
```

```python
import functools

import jax
import jax.numpy as jnp
from jax import lax
from jax.experimental import pallas as pl
from jax.experimental.pallas import tpu as pltpu

D_MODEL = 2048
D_SGU = 1024
SGU_GROUPS = 8
SGU_GROUP_DIM = 128
SGU_CHUNK = 128
D_DN = 1024
DN_HEADS = 8
DN_HEAD_DIM = 128
DN_CHUNK = 128
CONV_WIDTH = 5
CONV_PAD = CONV_WIDTH // 2
NORM_EPS = 1e-6
D_MAIN = 3 * D_SGU + 4 * D_DN
N_GATE = 4 * DN_HEADS
LANES = 128
SUBLANES = 8
BF16_ROWS = 16
COL_BLOCK = 1024

CB_U, CB_V, CB_ZA, CB_Q, CB_ZB = 0, 1, 2, 3, 6
GC_FWD, GC_BWD, BETA_FWD, BETA_BWD = 0, DN_HEADS, 2 * DN_HEADS, 3 * DN_HEADS

F32 = jnp.float32
BF16 = jnp.bfloat16
HIGHEST = lax.Precision.HIGHEST

TM_IN = 1024
TN_IN = 1792
R_GATE = 1024
R_PREP = 512
PREP_TILE = 128
TM_OUT = 512
DELTA_NCH = 4
SOLVE_BLOCK = 32
DELTA_SKEW = 6
W_IN_CAST_ROWS = 800
W_OUT_CAST_ROWS = 512
VMEM_LIMIT = 52 * 1024 * 1024


def _sigmoid(x):
    return 1.0 / (1.0 + jnp.exp(-x))


def _gelu(x):
    return 0.5 * x * (1.0 + lax.erf(x * (2.0 ** -0.5)))


def _mm(a, b):
    return jnp.dot(a.astype(BF16), b.astype(BF16), preferred_element_type=F32)


def _nt_dot(a, b):
    return lax.dot_general(a, b, (((1,), (1,)), ((), ())), preferred_element_type=F32)


def _tn_dot(a, b):
    return lax.dot_general(a, b, (((0,), (0,)), ((), ())), preferred_element_type=F32)


def _inproj_kernel(x_ref, nw_ref, w_ref, wg_ref, out_ref, gate_ref, hb_ref):
    @pl.when(pl.program_id(1) == 0)
    def _():
        x = x_ref[...]
        ms = jnp.mean(x * x, axis=-1, keepdims=True)
        h = x * lax.rsqrt(ms + NORM_EPS) * nw_ref[...]
        hb = h.astype(BF16)
        hb_ref[...] = hb
        h_lo = (h - hb.astype(F32)).astype(BF16)
        g_hi = _nt_dot(hb, wg_ref[0])
        g_lo = _nt_dot(h_lo, wg_ref[0, 0:LANES, :])
        gate_ref[...] = g_hi[:, 0:LANES] + g_hi[:, LANES:2 * LANES] + g_lo

    out_ref[...] = _nt_dot(hb_ref[...], w_ref[0]).astype(BF16)


def _inproj(x2, norm_w, w_all, wg_all, layer):
    t = x2.shape[0]
    return pl.pallas_call(
        _inproj_kernel,
        grid=(t // TM_IN, D_MAIN // TN_IN),
        in_specs=[
            pl.BlockSpec((TM_IN, D_MODEL), lambda i, j: (i, 0)),
            pl.BlockSpec((1, D_MODEL), lambda i, j: (0, 0)),
            pl.BlockSpec((1, TN_IN, D_MODEL), lambda i, j: (layer, j, 0)),
            pl.BlockSpec((1, 2 * LANES, D_MODEL), lambda i, j: (layer, 0, 0)),
        ],
        out_specs=[
            pl.BlockSpec((TM_IN, TN_IN), lambda i, j: (i, j)),
            pl.BlockSpec((TM_IN, LANES), lambda i, j: (i, 0)),
        ],
        out_shape=[
            jax.ShapeDtypeStruct((t, D_MAIN), BF16),
            jax.ShapeDtypeStruct((t, LANES), F32),
        ],
        scratch_shapes=[pltpu.VMEM((TM_IN, D_MODEL), BF16)],
        compiler_params=pltpu.CompilerParams(
            dimension_semantics=("parallel", "arbitrary"),
            vmem_limit_bytes=VMEM_LIMIT),
        name="inproj",
    )(x2, norm_w, w_all, wg_all)


def _gate_kernel(g_ref, alog_ref, dtb_ref, gc_ref, gt_ref):
    r = g_ref.shape[0]
    c = DN_CHUNK
    lane = lax.broadcasted_iota(jnp.int32, (c, LANES), 1)
    ri = lax.broadcasted_iota(jnp.int32, (c, c), 0)
    ci = lax.broadcasted_iota(jnp.int32, (c, c), 1)
    lower = jnp.where(ri >= ci, 1.0, 0.0).astype(F32)
    upper = jnp.where(ri <= ci, 1.0, 0.0).astype(F32)
    neg_a = -jnp.exp(alog_ref[...])
    for cc in range(r // c):
        rows = slice(cc * c, (cc + 1) * c)
        raw = g_ref[rows, :]
        sp_in = raw + dtb_ref[...]
        softplus = jnp.maximum(sp_in, 0.0) + jnp.log1p(jnp.exp(-jnp.abs(sp_in)))
        g = jnp.where(lane < BETA_FWD, neg_a * softplus, 0.0)
        beta = _sigmoid(raw)
        pre = jnp.dot(lower, g, precision=HIGHEST, preferred_element_type=F32)
        suf = jnp.dot(upper, g, precision=HIGHEST, preferred_element_type=F32)
        out = jnp.where(lane < GC_BWD, pre,
                        jnp.where(lane < BETA_FWD, suf,
                                  jnp.where(lane < N_GATE, beta, 0.0)))
        gc_ref[rows, :] = out
        gt_ref[cc] = out.T[0:N_GATE, :]


def _gates(gates_raw, alog_row, dtb_row):
    t = gates_raw.shape[0]
    return pl.pallas_call(
        _gate_kernel,
        grid=(t // R_GATE,),
        in_specs=[
            pl.BlockSpec((R_GATE, LANES), lambda i: (i, 0)),
            pl.BlockSpec((1, LANES), lambda i: (0, 0)),
            pl.BlockSpec((1, LANES), lambda i: (0, 0)),
        ],
        out_specs=[
            pl.BlockSpec((R_GATE, LANES), lambda i: (i, 0)),
            pl.BlockSpec((R_GATE // DN_CHUNK, N_GATE, DN_CHUNK), lambda i: (i, 0, 0)),
        ],
        out_shape=[
            jax.ShapeDtypeStruct((t, LANES), F32),
            jax.ShapeDtypeStruct((t // DN_CHUNK, N_GATE, DN_CHUNK), F32),
        ],
        compiler_params=pltpu.CompilerParams(dimension_semantics=("parallel",)),
        name="gates",
    )(gates_raw, alog_row, dtb_row)


def _sgu_rows(u_ref, v_ref, z_ref, lg_ref, lb_ref, ws_ref, bs_ref, o_ref, vn_ref):
    r = u_ref.shape[0]
    v = _gelu(v_ref[...].astype(F32))
    mu = jnp.mean(v, axis=-1, keepdims=True)
    vc = v - mu
    var = jnp.mean(vc * vc, axis=-1, keepdims=True)
    vn = vc * lax.rsqrt(var + NORM_EPS) * lg_ref[...] + lb_ref[...]
    vn_ref[...] = vn.astype(BF16)
    for c in range(r // SGU_CHUNK):
        rows = slice(c * SGU_CHUNK, (c + 1) * SGU_CHUNK)
        for g in range(SGU_GROUPS):
            cols = slice(g * SGU_GROUP_DIM, (g + 1) * SGU_GROUP_DIM)
            sp = jnp.dot(ws_ref[0, g], vn_ref[rows, cols], preferred_element_type=F32)
            sp = sp + bs_ref[:, g:g + 1]
            u = _gelu(u_ref[rows, cols].astype(F32))
            z = z_ref[rows, cols].astype(F32)
            o_ref[rows, cols] = (u * sp * (z * _sigmoid(z))).astype(BF16)


def _prep_kernel(x_ref, p_ref, n_ref, cw_ref, o_ref, pad_ref, *, normalize):
    kind = pl.program_id(0)
    i = pl.program_id(2)
    r = x_ref.shape[1]
    halo = BF16_ROWS
    zeros = jnp.zeros((halo, COL_BLOCK), BF16)
    pad_ref[0:halo, :] = jnp.where(i > 0, p_ref[0], zeros)
    pad_ref[halo:halo + r, :] = x_ref[0]
    pad_ref[halo + r:2 * halo + r, :] = jnp.where(i < pl.num_programs(2) - 1, n_ref[0], zeros)
    tile = PREP_TILE
    kdim = tile + 2 * halo
    ri = lax.broadcasted_iota(jnp.int32, (tile, kdim), 0)
    ci = lax.broadcasted_iota(jnp.int32, (tile, kdim), 1)
    taps = [j for j in range(CONV_WIDTH) if j != CONV_PAD]
    shift = jnp.concatenate(
        [jnp.where(ci == ri + halo + (j - CONV_PAD), 1.0, 0.0).astype(BF16) for j in taps],
        axis=0)
    scale = jnp.where(kind == 0, DN_HEAD_DIM ** -0.5, 1.0).astype(F32)
    for a in range(r // tile):
        rows = slice(a * tile, (a + 1) * tile)
        xs = pad_ref[a * tile:a * tile + kdim, :]
        shifted = jnp.dot(shift, xs, preferred_element_type=F32)
        acc = cw_ref[CONV_PAD:CONV_PAD + 1, :] * xs[halo:halo + tile, :].astype(F32)
        for n, j in enumerate(taps):
            acc = acc + cw_ref[j:j + 1, :] * shifted[n * tile:(n + 1) * tile, :]
        y = acc * _sigmoid(acc)
        if not normalize:
            o_ref[0, 0, rows, :] = y
            continue
        for h in range(DN_HEADS):
            cols = slice(h * DN_HEAD_DIM, (h + 1) * DN_HEAD_DIM)
            yh = y[:, cols]
            ss = jnp.sum(yh * yh, axis=-1, keepdims=True)
            o_ref[0, 0, rows, cols] = yh * (lax.rsqrt(ss + NORM_EPS) * scale)


def _prep(proj3, conv_w, *, first_block, n_blocks, normalize):
    b, s, _ = proj3.shape
    nhalo = s // BF16_ROWS
    rb = R_PREP // BF16_ROWS
    cw0 = first_block - CB_Q
    return pl.pallas_call(
        functools.partial(_prep_kernel, normalize=normalize),
        grid=(n_blocks, b, s // R_PREP),
        in_specs=[
            pl.BlockSpec((1, R_PREP, COL_BLOCK), lambda kd, bb, i: (bb, i, first_block + kd)),
            pl.BlockSpec((1, BF16_ROWS, COL_BLOCK),
                         lambda kd, bb, i: (bb, jnp.maximum(i * rb - 1, 0), first_block + kd)),
            pl.BlockSpec((1, BF16_ROWS, COL_BLOCK),
                         lambda kd, bb, i: (bb, jnp.minimum((i + 1) * rb, nhalo - 1),
                                            first_block + kd)),
            pl.BlockSpec((CONV_WIDTH, COL_BLOCK), lambda kd, bb, i: (0, cw0 + kd)),
        ],
        out_specs=pl.BlockSpec((1, 1, R_PREP, D_DN), lambda kd, bb, i: (kd, bb, i, 0)),
        out_shape=jax.ShapeDtypeStruct((n_blocks, b, s, D_DN), F32),
        scratch_shapes=[pltpu.VMEM((R_PREP + 2 * BF16_ROWS, COL_BLOCK), BF16)],
        compiler_params=pltpu.CompilerParams(
            dimension_semantics=("parallel", "parallel", "parallel"),
            vmem_limit_bytes=VMEM_LIMIT),
        name="prep_qk" if normalize else "prep_v",
    )(proj3, proj3, proj3, conv_w)


def _tri_solve(ms, rhs_rows, width, upper, eye, ri, ci):
    n = range(len(ms))
    nblk = DN_CHUNK // SOLVE_BLOCK
    mb = [x.astype(BF16) for x in ms]
    d = yield from _tri_inverse(ms, eye, ri, ci, SOLVE_BLOCK)
    db = [x.astype(BF16) for x in d]
    zero = jnp.zeros((SOLVE_BLOCK, width), BF16)
    order = [list(range(nblk - 1, -1, -1)) if upper[i] else list(range(nblk)) for i in n]
    x = [[None] * nblk for _ in n]
    xf = [[None] * nblk for _ in n]

    def place(blocks):
        return jnp.concatenate([zero if b is None else b for b in blocks], axis=0)

    for step in range(nblk):
        rows = [slice(order[i][step] * SOLVE_BLOCK, (order[i][step] + 1) * SOLVE_BLOCK)
                for i in n]
        z = [rhs_rows(i, rows[i]) for i in n]
        if step > 0:
            acc = [jnp.dot(mb[i][rows[i], :], place(x[i]), preferred_element_type=F32)
                   for i in n]
            yield
            z = [z[i] - acc[i] for i in n]
        zfull = []
        for i in n:
            blocks = [None] * nblk
            blocks[order[i][step]] = z[i].astype(BF16)
            zfull.append(place(blocks))
        xr = [jnp.dot(db[i][rows[i], :], zfull[i], preferred_element_type=F32) for i in n]
        yield
        for i in n:
            xf[i][order[i][step]] = xr[i]
            x[i][order[i][step]] = xr[i].astype(BF16)
    return [jnp.concatenate(xf[i], axis=0) for i in n]


def _tri_inverse(ms, eye, ri, ci, size):
    blk = SUBLANES
    same = (ri // blk) == (ci // blk)
    p = [jnp.where(same, m, 0.0).astype(BF16) for m in ms]
    p2 = [_mm(x, x) for x in p]
    yield
    p2b = [x.astype(BF16) for x in p2]
    p34 = [_mm(y, jnp.concatenate([x, y], axis=1)) for x, y in zip(p, p2b)]
    yield
    d = [eye - x + y - z[:, :DN_CHUNK] for x, y, z in zip(p, p2, p34)]
    d = [x + _mm(z[:, DN_CHUNK:], x) for x, z in zip(d, p34)]
    yield
    while blk < size:
        same2 = (ri // (2 * blk)) == (ci // (2 * blk))
        sel = same2 & jnp.logical_not(same)
        off = [jnp.where(sel, m, 0.0).astype(BF16) for m in ms]
        db = [x.astype(BF16) for x in d]
        t = [_mm(x, y) for x, y in zip(db, off)]
        yield
        d = [x - _mm(y, z) for x, y, z in zip(d, t, db)]
        yield
        same = same2
        blk *= 2
    return d


def _delta_kernel(qf_ref, kf_ref, vf_ref, gcf_ref, gtf_ref,
                  qb_ref, kb_ref, vb_ref, gcb_ref, gtb_ref,
                  of_ref, ob_ref, s_ref):
    @pl.when(pl.program_id(1) == 0)
    def _():
        s_ref[...] = jnp.zeros_like(s_ref)

    c = DN_CHUNK
    ri = lax.broadcasted_iota(jnp.int32, (c, c), 0)
    ci = lax.broadcasted_iota(jnp.int32, (c, c), 1)
    eye = jnp.where(ri == ci, 1.0, 0.0).astype(F32)
    fwd = dict(q=qf_ref, k=kf_ref, v=vf_ref, gc=gcf_ref, gt=gtf_ref, o=of_ref,
               incl=ri >= ci, strict=ri > ci, g_off=GC_FWD, b_off=BETA_FWD, last=c - 1,
               s0=0, order=list(range(DELTA_NCH)))
    bwd = dict(q=qb_ref, k=kb_ref, v=vb_ref, gc=gcb_ref, gt=gtb_ref, o=ob_ref,
               incl=ri <= ci, strict=ri < ci, g_off=GC_BWD, b_off=BETA_BWD, last=0,
               s0=DN_HEADS, order=list(range(DELTA_NCH - 1, -1, -1)))
    state = {(d["s0"], h): s_ref[d["s0"] + h] for d in (fwd, bwd) for h in range(DN_HEADS)}
    done = set()
    waves = [_delta_wave([(d, t, h) for d in (fwd, bwd) for h in range(DN_HEADS)],
                         (t, 0), (t - 1, 0), bwd, state, done, eye, ri, ci)
             for t in range(DELTA_NCH)]
    pending, active, tick = list(waves), [], 0
    while pending or active:
        if pending and tick % DELTA_SKEW == 0:
            active.append(pending.pop(0))
        for g in list(active):
            try:
                next(g)
            except StopIteration:
                active.remove(g)
        tick += 1
    for (s0, h), val in state.items():
        s_ref[s0 + h] = val


def _delta_wave(probs, wave, prev_wave, bwd, state, done, eye, ri, ci):
    c = DN_CHUNK
    n = range(len(probs))
    rows = [slice(d["order"][t] * c, (d["order"][t] + 1) * c) for d, t, h in probs]
    cols = [slice(h * DN_HEAD_DIM, (h + 1) * DN_HEAD_DIM) for d, t, h in probs]
    q = [d["q"][0, 0, rows[i], cols[i]] for i, (d, t, h) in enumerate(probs)]
    k = [d["k"][0, 0, rows[i], cols[i]] for i, (d, t, h) in enumerate(probs)]
    v = [d["v"][0, 0, rows[i], cols[i]] for i, (d, t, h) in enumerate(probs)]
    gcol = [d["gc"][0, rows[i], d["g_off"] + h:d["g_off"] + h + 1]
            for i, (d, t, h) in enumerate(probs)]
    bcol = [d["gc"][0, rows[i], d["b_off"] + h:d["b_off"] + h + 1]
            for i, (d, t, h) in enumerate(probs)]
    grow = [d["gt"][0, d["order"][t], d["g_off"] + h:d["g_off"] + h + 1, :]
            for d, t, h in probs]
    glast = [grow[i][:, d["last"]:d["last"] + 1] for i, (d, t, h) in enumerate(probs)]
    kbeta = [k[i] * bcol[i] for i in n]
    kb = [k[i].astype(BF16) for i in n]
    kq_lhs = [jnp.concatenate([kbeta[i], q[i]], axis=0).astype(BF16) for i in n]
    yield
    kq = [_nt_dot(kq_lhs[i], kb[i]) for i in n]
    yield
    decay = [jnp.where(d["incl"], jnp.exp(gcol[i] - grow[i]), 0.0)
             for i, (d, t, h) in enumerate(probs)]
    m = [jnp.where(d["strict"], kq[i][:c] * decay[i], 0.0) for i, (d, t, h) in enumerate(probs)]
    attn = [(kq[i][c:] * decay[i]).astype(BF16) for i in n]
    eg = [jnp.exp(gcol[i]) for i in n]
    rhs = [jnp.concatenate([v[i] * bcol[i], kbeta[i] * eg[i]], axis=1) for i in n]
    upper = [d is bwd for d, t, h in probs]
    uw = yield from _tri_solve(m, lambda i, sub: rhs[i][sub, :], 2 * DN_HEAD_DIM, upper,
                               eye, ri, ci)
    lhs = [jnp.concatenate([uw[i][:, DN_HEAD_DIM:], q[i] * eg[i]], axis=0).astype(BF16)
           for i in n]
    k_dec = [(k[i] * jnp.exp(glast[i] - gcol[i])).astype(BF16) for i in n]
    g_chunk = [jnp.exp(glast[i]) for i in n]
    assert prev_wave[0] < 0 or prev_wave in done
    key = [(d["s0"], h) for d, t, h in probs]
    ws = [_mm(lhs[i], state[key[i]]) for i in n]
    yield
    v_new = [(uw[i][:, :DN_HEAD_DIM] - ws[i][:c]).astype(BF16) for i in n]
    o = [ws[i][c:] + jnp.dot(attn[i], v_new[i], preferred_element_type=F32) for i in n]
    s_new = [state[key[i]] * g_chunk[i] + _tn_dot(k_dec[i], v_new[i]) for i in n]
    yield
    for i, (d, t, h) in enumerate(probs):
        state[key[i]] = s_new[i]
        d["o"][0, rows[i], cols[i]] = o[i].astype(BF16)
    done.add(wave)


def _delta(qkn, vc, gc3, gt4):
    _, b, s, _ = qkn.shape
    r = DELTA_NCH * DN_CHUNK
    nb = s // r

    def dir_specs(bidx):
        return [
            pl.BlockSpec((1, 1, r, D_DN), lambda bb, i: (0, bb, bidx(i), 0)),
            pl.BlockSpec((1, 1, r, D_DN), lambda bb, i: (1, bb, bidx(i), 0)),
            pl.BlockSpec((1, 1, r, D_DN), lambda bb, i: (0, bb, bidx(i), 0)),
            pl.BlockSpec((1, r, LANES), lambda bb, i: (bb, bidx(i), 0)),
            pl.BlockSpec((1, DELTA_NCH, N_GATE, DN_CHUNK), lambda bb, i: (bb, bidx(i), 0, 0)),
        ]

    up = lambda i: i
    down = lambda i: nb - 1 - i
    return pl.pallas_call(
        _delta_kernel,
        grid=(b, nb),
        in_specs=dir_specs(up) + dir_specs(down),
        out_specs=[
            pl.BlockSpec((1, r, D_DN), lambda bb, i: (bb, up(i), 0)),
            pl.BlockSpec((1, r, D_DN), lambda bb, i: (bb, down(i), 0)),
        ],
        out_shape=[jax.ShapeDtypeStruct((b, s, D_DN), BF16)] * 2,
        scratch_shapes=[pltpu.VMEM((2 * DN_HEADS, DN_HEAD_DIM, DN_HEAD_DIM), F32)],
        compiler_params=pltpu.CompilerParams(
            dimension_semantics=("parallel", "arbitrary"),
            vmem_limit_bytes=VMEM_LIMIT),
        name="delta",
    )(qkn, qkn, vc, gc3, gt4, qkn, qkn, vc, gc3, gt4)


def _outproj_kernel(x_ref, u_ref, v_ref, za_ref, of_ref, ob_ref, zb_ref, lg_ref, lb_ref,
                    ws_ref, bs_ref, nw_ref, w_ref, fw_ref, o_ref, y_ref, vn_ref, *, final):
    _sgu_rows(u_ref, v_ref, za_ref, lg_ref, lb_ref, ws_ref, bs_ref, y_ref, vn_ref)
    for h in range(DN_HEADS):
        cols = slice(h * DN_HEAD_DIM, (h + 1) * DN_HEAD_DIM)
        o = of_ref[:, cols].astype(F32) + ob_ref[:, cols].astype(F32)
        ms = jnp.mean(o * o, axis=-1, keepdims=True)
        z = zb_ref[:, cols].astype(F32)
        yh = o * lax.rsqrt(ms + NORM_EPS) * nw_ref[...] * (z * _sigmoid(z))
        y_ref[:, D_SGU + h * DN_HEAD_DIM:D_SGU + (h + 1) * DN_HEAD_DIM] = yh.astype(BF16)
    xn = x_ref[...] + jnp.dot(y_ref[...], w_ref[0], preferred_element_type=F32)
    if final:
        ms = jnp.mean(xn * xn, axis=-1, keepdims=True)
        xn = xn * lax.rsqrt(ms + NORM_EPS) * fw_ref[...]
    o_ref[...] = xn


def _outproj(x2, o_f, o_b, proj, ln_g, ln_b, ws_all, bs_t, dn_norm_w, w_out_all, final_w,
             layer, *, final):
    t = x2.shape[0]
    return pl.pallas_call(
        functools.partial(_outproj_kernel, final=final),
        grid=(t // TM_OUT,),
        in_specs=[
            pl.BlockSpec((TM_OUT, D_MODEL), lambda i: (i, 0)),
            pl.BlockSpec((TM_OUT, COL_BLOCK), lambda i: (i, CB_U)),
            pl.BlockSpec((TM_OUT, COL_BLOCK), lambda i: (i, CB_V)),
            pl.BlockSpec((TM_OUT, COL_BLOCK), lambda i: (i, CB_ZA)),
            pl.BlockSpec((TM_OUT, D_DN), lambda i: (i, 0)),
            pl.BlockSpec((TM_OUT, D_DN), lambda i: (i, 0)),
            pl.BlockSpec((TM_OUT, COL_BLOCK), lambda i: (i, CB_ZB)),
            pl.BlockSpec((1, D_SGU), lambda i: (0, 0)),
            pl.BlockSpec((1, D_SGU), lambda i: (0, 0)),
            pl.BlockSpec((1, SGU_GROUPS, SGU_CHUNK, SGU_CHUNK), lambda i: (layer, 0, 0, 0)),
            pl.BlockSpec((SGU_CHUNK, SGU_GROUPS), lambda i: (0, 0)),
            pl.BlockSpec((1, DN_HEAD_DIM), lambda i: (0, 0)),
            pl.BlockSpec((1, D_MODEL, D_MODEL), lambda i: (layer, 0, 0)),
            pl.BlockSpec((1, D_MODEL), lambda i: (0, 0)),
        ],
        out_specs=pl.BlockSpec((TM_OUT, D_MODEL), lambda i: (i, 0)),
        out_shape=jax.ShapeDtypeStruct((t, D_MODEL), F32),
        scratch_shapes=[pltpu.VMEM((TM_OUT, D_MODEL), BF16),
                        pltpu.VMEM((TM_OUT, D_SGU), BF16)],
        compiler_params=pltpu.CompilerParams(
            dimension_semantics=("parallel",), vmem_limit_bytes=VMEM_LIMIT),
        name="outproj_final" if final else "outproj",
    )(x2, proj, proj, proj, o_f, o_b, proj, ln_g, ln_b, ws_all, bs_t, dn_norm_w, w_out_all,
      final_w)


def _cast_kernel(w_ref, o_ref):
    o_ref[...] = w_ref[...].astype(BF16)


def _cast_bf16(w, rows_per_step):
    nl, rows, cols = w.shape
    assert rows % rows_per_step == 0 and rows_per_step % BF16_ROWS == 0
    return pl.pallas_call(
        _cast_kernel,
        grid=(nl, rows // rows_per_step),
        in_specs=[pl.BlockSpec((1, rows_per_step, cols), lambda l, i: (l, i, 0))],
        out_specs=pl.BlockSpec((1, rows_per_step, cols), lambda l, i: (l, i, 0)),
        out_shape=jax.ShapeDtypeStruct(w.shape, BF16),
        compiler_params=pltpu.CompilerParams(
            dimension_semantics=("parallel", "parallel"), vmem_limit_bytes=VMEM_LIMIT),
        name="cast_bf16",
    )(w)


def _pad_lanes(row):
    return jnp.pad(row, (0, LANES - row.shape[0]))[None, :]


def kernel(x, norm_w, w_in, sgu_ln_g, sgu_ln_b, sgu_w, sgu_b, conv_w, a_log_f, a_log_b,
           dt_bias_f, dt_bias_b, dn_norm_w, w_out, final_norm_w):
    b, s, _ = x.shape
    t = b * s
    depth = w_in.shape[0]
    x2 = x.reshape(t, D_MODEL)
    w_t = jnp.swapaxes(w_in, 1, 2)
    w_all = _cast_bf16(w_t, W_IN_CAST_ROWS)
    w_gate = jnp.pad(w_t[:, D_MAIN:, :], ((0, 0), (0, LANES - N_GATE), (0, 0)))
    wg_hi = w_gate.astype(BF16)
    wg_lo = (w_gate - wg_hi.astype(F32)).astype(BF16)
    wg_all = jnp.concatenate([wg_hi, wg_lo], axis=1)
    ws_all = sgu_w.astype(BF16)
    w_out_all = _cast_bf16(w_out, W_OUT_CAST_ROWS)
    for l in range(depth):
        proj, gates_raw = _inproj(x2, norm_w[l][None, :], w_all, wg_all, l)
        alog_row = _pad_lanes(jnp.concatenate([a_log_f[l], a_log_b[l]]))
        dtb_row = _pad_lanes(jnp.concatenate([dt_bias_f[l], dt_bias_b[l]]))
        gc, gt = _gates(gates_raw, alog_row, dtb_row)
        proj3 = proj.reshape(b, s, D_MAIN)
        qkn = _prep(proj3, conv_w[l], first_block=CB_Q, n_blocks=2, normalize=True)
        vc = _prep(proj3, conv_w[l], first_block=CB_Q + 2, n_blocks=1, normalize=False)
        gc3 = gc.reshape(b, s, LANES)
        gt4 = gt.reshape(b, s // DN_CHUNK, N_GATE, DN_CHUNK)
        o_f, o_b = _delta(qkn, vc, gc3, gt4)
        x2 = _outproj(x2, o_f.reshape(t, D_DN), o_b.reshape(t, D_DN), proj,
                      sgu_ln_g[l][None, :], sgu_ln_b[l][None, :], ws_all, sgu_b[l].T,
                      dn_norm_w[l][None, :], w_out_all, final_norm_w[None, :], l,
                      final=(l == depth - 1))
    return x2.reshape(b, s, D_MODEL)
```

```python
import functools

import jax
import jax.numpy as jnp
from jax import lax
from jax.experimental import pallas as pl
from jax.experimental.pallas import tpu as pltpu

D_MODEL = 2048
D_SGU = 1024
SGU_GROUPS = 8
SGU_GROUP_DIM = 128
SGU_CHUNK = 128
D_DN = 1024
DN_HEADS = 8
DN_HEAD_DIM = 128
DN_CHUNK = 128
CONV_WIDTH = 5
CONV_PAD = CONV_WIDTH // 2
NORM_EPS = 1e-6
D_MAIN = 3 * D_SGU + 4 * D_DN
N_GATE = 4 * DN_HEADS
LANES = 128
SUBLANES = 8
BF16_ROWS = 16
COL_BLOCK = 1024

CB_U, CB_V, CB_ZA, CB_Q, CB_ZB = 0, 1, 2, 3, 6
GC_FWD, GC_BWD, BETA_FWD, BETA_BWD = 0, DN_HEADS, 2 * DN_HEADS, 3 * DN_HEADS

F32 = jnp.float32
BF16 = jnp.bfloat16
HIGHEST = lax.Precision.HIGHEST

TM_IN = 1024
TN_IN = 1792
R_GATE = 1024
R_PREP = 512
PREP_TILE = 128
TM_OUT = 512
DELTA_NCH = 4
SOLVE_BLOCK = 32
DELTA_SKEW = 6
W_IN_CAST_ROWS = 800
W_OUT_CAST_ROWS = 512
VMEM_LIMIT = 52 * 1024 * 1024


def _sigmoid(x):
    return 1.0 / (1.0 + jnp.exp(-x))


def _gelu(x):
    return 0.5 * x * (1.0 + lax.erf(x * (2.0 ** -0.5)))


def _mm(a, b):
    return jnp.dot(a.astype(BF16), b.astype(BF16), preferred_element_type=F32)


def _nt_dot(a, b):
    return lax.dot_general(a, b, (((1,), (1,)), ((), ())), preferred_element_type=F32)


def _tn_dot(a, b):
    return lax.dot_general(a, b, (((0,), (0,)), ((), ())), preferred_element_type=F32)


def _silu(x):
    return x * _sigmoid(x)


_PROJ_ACTIVATIONS = (
    (0, 2 * D_SGU, _gelu),
    (2 * D_SGU, 3 * D_SGU, _silu),
    (3 * D_SGU, 3 * D_SGU + 3 * D_DN, None),
    (3 * D_SGU + 3 * D_DN, D_MAIN, _silu),
)


def _tile_activations(j):
    base = j * TN_IN
    pieces = []
    for start, stop, fn in _PROJ_ACTIVATIONS:
        lo, hi = max(start, base), min(stop, base + TN_IN)
        if lo < hi:
            pieces.append((lo - base, hi - base, fn if fn is not None else (lambda t: t)))
    return pieces


def _inproj_kernel(x_ref, nw_ref, w_ref, wg_ref, out_ref, gate_ref, hb_ref):
    @pl.when(pl.program_id(1) == 0)
    def _():
        x = x_ref[...]
        ms = jnp.mean(x * x, axis=-1, keepdims=True)
        h = x * lax.rsqrt(ms + NORM_EPS) * nw_ref[...]
        hb = h.astype(BF16)
        hb_ref[...] = hb
        h_lo = (h - hb.astype(F32)).astype(BF16)
        g_hi = _nt_dot(hb, wg_ref[0])
        g_lo = _nt_dot(h_lo, wg_ref[0, 0:LANES, :])
        gate_ref[...] = g_hi[:, 0:LANES] + g_hi[:, LANES:2 * LANES] + g_lo

    for jj in range(D_MAIN // TN_IN):
        @pl.when(pl.program_id(1) == jj)
        def _(jj=jj):
            y = _nt_dot(hb_ref[...], w_ref[0])
            for lo, hi, fn in _tile_activations(jj):
                out_ref[:, lo:hi] = fn(y[:, lo:hi]).astype(BF16)


def _inproj(x2, norm_w, w_all, wg_all, layer):
    t = x2.shape[0]
    return pl.pallas_call(
        _inproj_kernel,
        grid=(t // TM_IN, D_MAIN // TN_IN),
        in_specs=[
            pl.BlockSpec((TM_IN, D_MODEL), lambda i, j: (i, 0)),
            pl.BlockSpec((1, D_MODEL), lambda i, j: (0, 0)),
            pl.BlockSpec((1, TN_IN, D_MODEL), lambda i, j: (layer, j, 0)),
            pl.BlockSpec((1, 2 * LANES, D_MODEL), lambda i, j: (layer, 0, 0)),
        ],
        out_specs=[
            pl.BlockSpec((TM_IN, TN_IN), lambda i, j: (i, j)),
            pl.BlockSpec((TM_IN, LANES), lambda i, j: (i, 0)),
        ],
        out_shape=[
            jax.ShapeDtypeStruct((t, D_MAIN), BF16),
            jax.ShapeDtypeStruct((t, LANES), F32),
        ],
        scratch_shapes=[pltpu.VMEM((TM_IN, D_MODEL), BF16)],
        compiler_params=pltpu.CompilerParams(
            dimension_semantics=("parallel", "arbitrary"),
            vmem_limit_bytes=VMEM_LIMIT),
        name="inproj",
    )(x2, norm_w, w_all, wg_all)


def _gate_kernel(g_ref, alog_ref, dtb_ref, gc_ref, gt_ref):
    r = g_ref.shape[0]
    c = DN_CHUNK
    lane = lax.broadcasted_iota(jnp.int32, (c, LANES), 1)
    ri = lax.broadcasted_iota(jnp.int32, (c, c), 0)
    ci = lax.broadcasted_iota(jnp.int32, (c, c), 1)
    lower = jnp.where(ri >= ci, 1.0, 0.0).astype(F32)
    upper = jnp.where(ri <= ci, 1.0, 0.0).astype(F32)
    neg_a = -jnp.exp(alog_ref[...])
    for cc in range(r // c):
        rows = slice(cc * c, (cc + 1) * c)
        raw = g_ref[rows, :]
        sp_in = raw + dtb_ref[...]
        softplus = jnp.maximum(sp_in, 0.0) + jnp.log1p(jnp.exp(-jnp.abs(sp_in)))
        g = jnp.where(lane < BETA_FWD, neg_a * softplus, 0.0)
        beta = _sigmoid(raw)
        pre = jnp.dot(lower, g, precision=HIGHEST, preferred_element_type=F32)
        suf = jnp.dot(upper, g, precision=HIGHEST, preferred_element_type=F32)
        out = jnp.where(lane < GC_BWD, pre,
                        jnp.where(lane < BETA_FWD, suf,
                                  jnp.where(lane < N_GATE, beta, 0.0)))
        gc_ref[rows, :] = out
        gt_ref[cc] = out.T[0:N_GATE, :]


def _gates(gates_raw, alog_row, dtb_row):
    t = gates_raw.shape[0]
    return pl.pallas_call(
        _gate_kernel,
        grid=(t // R_GATE,),
        in_specs=[
            pl.BlockSpec((R_GATE, LANES), lambda i: (i, 0)),
            pl.BlockSpec((1, LANES), lambda i: (0, 0)),
            pl.BlockSpec((1, LANES), lambda i: (0, 0)),
        ],
        out_specs=[
            pl.BlockSpec((R_GATE, LANES), lambda i: (i, 0)),
            pl.BlockSpec((R_GATE // DN_CHUNK, N_GATE, DN_CHUNK), lambda i: (i, 0, 0)),
        ],
        out_shape=[
            jax.ShapeDtypeStruct((t, LANES), F32),
            jax.ShapeDtypeStruct((t // DN_CHUNK, N_GATE, DN_CHUNK), F32),
        ],
        compiler_params=pltpu.CompilerParams(dimension_semantics=("parallel",)),
        name="gates",
    )(gates_raw, alog_row, dtb_row)


def _sgu_rows(u_ref, v_ref, z_ref, lg_ref, lb_ref, ws_ref, bs_ref, o_ref, vn_ref):
    r = u_ref.shape[0]
    v = v_ref[...].astype(F32)
    mu = jnp.mean(v, axis=-1, keepdims=True)
    vc = v - mu
    var = jnp.mean(vc * vc, axis=-1, keepdims=True)
    vn = vc * lax.rsqrt(var + NORM_EPS) * lg_ref[...] + lb_ref[...]
    vn_ref[...] = vn.astype(BF16)
    for c in range(r // SGU_CHUNK):
        rows = slice(c * SGU_CHUNK, (c + 1) * SGU_CHUNK)
        for g in range(SGU_GROUPS):
            cols = slice(g * SGU_GROUP_DIM, (g + 1) * SGU_GROUP_DIM)
            sp = jnp.dot(ws_ref[0, g], vn_ref[rows, cols], preferred_element_type=F32)
            sp = sp + bs_ref[:, g:g + 1]
            u = u_ref[rows, cols].astype(F32)
            z = z_ref[rows, cols].astype(F32)
            o_ref[rows, cols] = (u * sp * z).astype(BF16)


def _prep_kernel(x_ref, p_ref, n_ref, cw_ref, o_ref, pad_ref, *, normalize):
    kind = pl.program_id(0)
    i = pl.program_id(2)
    r = x_ref.shape[1]
    halo = BF16_ROWS
    zeros = jnp.zeros((halo, COL_BLOCK), BF16)
    pad_ref[0:halo, :] = jnp.where(i > 0, p_ref[0], zeros)
    pad_ref[halo:halo + r, :] = x_ref[0]
    pad_ref[halo + r:2 * halo + r, :] = jnp.where(i < pl.num_programs(2) - 1, n_ref[0], zeros)
    tile = PREP_TILE
    kdim = tile + 2 * halo
    ri = lax.broadcasted_iota(jnp.int32, (tile, kdim), 0)
    ci = lax.broadcasted_iota(jnp.int32, (tile, kdim), 1)
    taps = [j for j in range(CONV_WIDTH) if j != CONV_PAD]
    shift = jnp.concatenate(
        [jnp.where(ci == ri + halo + (j - CONV_PAD), 1.0, 0.0).astype(BF16) for j in taps],
        axis=0)
    scale = jnp.where(kind == 0, DN_HEAD_DIM ** -0.5, 1.0).astype(F32)
    for a in range(r // tile):
        rows = slice(a * tile, (a + 1) * tile)
        xs = pad_ref[a * tile:a * tile + kdim, :]
        shifted = jnp.dot(shift, xs, preferred_element_type=F32)
        acc = cw_ref[CONV_PAD:CONV_PAD + 1, :] * xs[halo:halo + tile, :].astype(F32)
        for n, j in enumerate(taps):
            acc = acc + cw_ref[j:j + 1, :] * shifted[n * tile:(n + 1) * tile, :]
        y = acc * _sigmoid(acc)
        if not normalize:
            o_ref[0, 0, rows, :] = y
            continue
        for h in range(DN_HEADS):
            cols = slice(h * DN_HEAD_DIM, (h + 1) * DN_HEAD_DIM)
            yh = y[:, cols]
            ss = jnp.sum(yh * yh, axis=-1, keepdims=True)
            o_ref[0, 0, rows, cols] = yh * (lax.rsqrt(ss + NORM_EPS) * scale)


def _prep(proj3, conv_w, *, first_block, n_blocks, normalize):
    b, s, _ = proj3.shape
    nhalo = s // BF16_ROWS
    rb = R_PREP // BF16_ROWS
    cw0 = first_block - CB_Q
    return pl.pallas_call(
        functools.partial(_prep_kernel, normalize=normalize),
        grid=(n_blocks, b, s // R_PREP),
        in_specs=[
            pl.BlockSpec((1, R_PREP, COL_BLOCK), lambda kd, bb, i: (bb, i, first_block + kd)),
            pl.BlockSpec((1, BF16_ROWS, COL_BLOCK),
                         lambda kd, bb, i: (bb, jnp.maximum(i * rb - 1, 0), first_block + kd)),
            pl.BlockSpec((1, BF16_ROWS, COL_BLOCK),
                         lambda kd, bb, i: (bb, jnp.minimum((i + 1) * rb, nhalo - 1),
                                            first_block + kd)),
            pl.BlockSpec((CONV_WIDTH, COL_BLOCK), lambda kd, bb, i: (0, cw0 + kd)),
        ],
        out_specs=pl.BlockSpec((1, 1, R_PREP, D_DN), lambda kd, bb, i: (kd, bb, i, 0)),
        out_shape=jax.ShapeDtypeStruct((n_blocks, b, s, D_DN), F32),
        scratch_shapes=[pltpu.VMEM((R_PREP + 2 * BF16_ROWS, COL_BLOCK), BF16)],
        compiler_params=pltpu.CompilerParams(
            dimension_semantics=("parallel", "parallel", "parallel"),
            vmem_limit_bytes=VMEM_LIMIT),
        name="prep_qk" if normalize else "prep_v",
    )(proj3, proj3, proj3, conv_w)


def _tri_solve(ms, rhs_rows, width, upper, eye, ri, ci):
    n = range(len(ms))
    nblk = DN_CHUNK // SOLVE_BLOCK
    mb = [x.astype(BF16) for x in ms]
    d = yield from _tri_inverse(ms, eye, ri, ci, SOLVE_BLOCK)
    db = [x.astype(BF16) for x in d]
    zero = jnp.zeros((SOLVE_BLOCK, width), BF16)
    order = [list(range(nblk - 1, -1, -1)) if upper[i] else list(range(nblk)) for i in n]
    x = [[None] * nblk for _ in n]
    xf = [[None] * nblk for _ in n]

    def place(blocks):
        return jnp.concatenate([zero if b is None else b for b in blocks], axis=0)

    for step in range(nblk):
        rows = [slice(order[i][step] * SOLVE_BLOCK, (order[i][step] + 1) * SOLVE_BLOCK)
                for i in n]
        z = [rhs_rows(i, rows[i]) for i in n]
        if step > 0:
            acc = [jnp.dot(mb[i][rows[i], :], place(x[i]), preferred_element_type=F32)
                   for i in n]
            yield
            z = [z[i] - acc[i] for i in n]
        zfull = []
        for i in n:
            blocks = [None] * nblk
            blocks[order[i][step]] = z[i].astype(BF16)
            zfull.append(place(blocks))
        xr = [jnp.dot(db[i][rows[i], :], zfull[i], preferred_element_type=F32) for i in n]
        yield
        for i in n:
            xf[i][order[i][step]] = xr[i]
            x[i][order[i][step]] = xr[i].astype(BF16)
    return [jnp.concatenate(xf[i], axis=0) for i in n]


def _tri_inverse(ms, eye, ri, ci, size):
    blk = SUBLANES
    same = (ri // blk) == (ci // blk)
    p = [jnp.where(same, m, 0.0).astype(BF16) for m in ms]
    p2 = [_mm(x, x) for x in p]
    yield
    p2b = [x.astype(BF16) for x in p2]
    p34 = [_mm(y, jnp.concatenate([x, y], axis=1)) for x, y in zip(p, p2b)]
    yield
    d = [eye - x + y - z[:, :DN_CHUNK] for x, y, z in zip(p, p2, p34)]
    d = [x + _mm(z[:, DN_CHUNK:], x) for x, z in zip(d, p34)]
    yield
    while blk < size:
        same2 = (ri // (2 * blk)) == (ci // (2 * blk))
        sel = same2 & jnp.logical_not(same)
        off = [jnp.where(sel, m, 0.0).astype(BF16) for m in ms]
        db = [x.astype(BF16) for x in d]
        t = [_mm(x, y) for x, y in zip(db, off)]
        yield
        d = [x - _mm(y, z) for x, y, z in zip(d, t, db)]
        yield
        same = same2
        blk *= 2
    return d


def _delta_kernel(qf_ref, kf_ref, vf_ref, gcf_ref, gtf_ref,
                  qb_ref, kb_ref, vb_ref, gcb_ref, gtb_ref,
                  of_ref, ob_ref, s_ref):
    @pl.when(pl.program_id(1) == 0)
    def _():
        s_ref[...] = jnp.zeros_like(s_ref)

    c = DN_CHUNK
    ri = lax.broadcasted_iota(jnp.int32, (c, c), 0)
    ci = lax.broadcasted_iota(jnp.int32, (c, c), 1)
    eye = jnp.where(ri == ci, 1.0, 0.0).astype(F32)
    fwd = dict(q=qf_ref, k=kf_ref, v=vf_ref, gc=gcf_ref, gt=gtf_ref, o=of_ref,
               incl=ri >= ci, strict=ri > ci, g_off=GC_FWD, b_off=BETA_FWD, last=c - 1,
               s0=0, order=list(range(DELTA_NCH)))
    bwd = dict(q=qb_ref, k=kb_ref, v=vb_ref, gc=gcb_ref, gt=gtb_ref, o=ob_ref,
               incl=ri <= ci, strict=ri < ci, g_off=GC_BWD, b_off=BETA_BWD, last=0,
               s0=DN_HEADS, order=list(range(DELTA_NCH - 1, -1, -1)))
    state = {(d["s0"], h): s_ref[d["s0"] + h] for d in (fwd, bwd) for h in range(DN_HEADS)}
    done = set()
    waves = [_delta_wave([(d, t, h) for d in (fwd, bwd) for h in range(DN_HEADS)],
                         (t, 0), (t - 1, 0), bwd, state, done, eye, ri, ci)
             for t in range(DELTA_NCH)]
    pending, active, tick = list(waves), [], 0
    while pending or active:
        if pending and tick % DELTA_SKEW == 0:
            active.append(pending.pop(0))
        for g in list(active):
            try:
                next(g)
            except StopIteration:
                active.remove(g)
        tick += 1
    for (s0, h), val in state.items():
        s_ref[s0 + h] = val


def _delta_wave(probs, wave, prev_wave, bwd, state, done, eye, ri, ci):
    c = DN_CHUNK
    n = range(len(probs))
    rows = [slice(d["order"][t] * c, (d["order"][t] + 1) * c) for d, t, h in probs]
    cols = [slice(h * DN_HEAD_DIM, (h + 1) * DN_HEAD_DIM) for d, t, h in probs]
    q = [d["q"][0, 0, rows[i], cols[i]] for i, (d, t, h) in enumerate(probs)]
    k = [d["k"][0, 0, rows[i], cols[i]] for i, (d, t, h) in enumerate(probs)]
    v = [d["v"][0, 0, rows[i], cols[i]] for i, (d, t, h) in enumerate(probs)]
    gcol = [d["gc"][0, rows[i], d["g_off"] + h:d["g_off"] + h + 1]
            for i, (d, t, h) in enumerate(probs)]
    bcol = [d["gc"][0, rows[i], d["b_off"] + h:d["b_off"] + h + 1]
            for i, (d, t, h) in enumerate(probs)]
    grow = [d["gt"][0, d["order"][t], d["g_off"] + h:d["g_off"] + h + 1, :]
            for d, t, h in probs]
    glast = [grow[i][:, d["last"]:d["last"] + 1] for i, (d, t, h) in enumerate(probs)]
    kbeta = [k[i] * bcol[i] for i in n]
    kb = [k[i].astype(BF16) for i in n]
    kq_lhs = [jnp.concatenate([kbeta[i], q[i]], axis=0).astype(BF16) for i in n]
    yield
    kq = [_nt_dot(kq_lhs[i], kb[i]) for i in n]
    yield
    decay = [jnp.where(d["incl"], jnp.exp(gcol[i] - grow[i]), 0.0)
             for i, (d, t, h) in enumerate(probs)]
    m = [jnp.where(d["strict"], kq[i][:c] * decay[i], 0.0) for i, (d, t, h) in enumerate(probs)]
    attn = [(kq[i][c:] * decay[i]).astype(BF16) for i in n]
    eg = [jnp.exp(gcol[i]) for i in n]
    rhs = [jnp.concatenate([v[i] * bcol[i], kbeta[i] * eg[i]], axis=1) for i in n]
    upper = [d is bwd for d, t, h in probs]
    uw = yield from _tri_solve(m, lambda i, sub: rhs[i][sub, :], 2 * DN_HEAD_DIM, upper,
                               eye, ri, ci)
    lhs = [jnp.concatenate([uw[i][:, DN_HEAD_DIM:], q[i] * eg[i]], axis=0).astype(BF16)
           for i in n]
    k_dec = [(k[i] * jnp.exp(glast[i] - gcol[i])).astype(BF16) for i in n]
    g_chunk = [jnp.exp(glast[i]) for i in n]
    assert prev_wave[0] < 0 or prev_wave in done
    key = [(d["s0"], h) for d, t, h in probs]
    ws = [_mm(lhs[i], state[key[i]]) for i in n]
    yield
    v_new = [(uw[i][:, :DN_HEAD_DIM] - ws[i][:c]).astype(BF16) for i in n]
    o = [ws[i][c:] + jnp.dot(attn[i], v_new[i], preferred_element_type=F32) for i in n]
    s_new = [state[key[i]] * g_chunk[i] + _tn_dot(k_dec[i], v_new[i]) for i in n]
    yield
    for i, (d, t, h) in enumerate(probs):
        state[key[i]] = s_new[i]
        d["o"][0, rows[i], cols[i]] = o[i].astype(BF16)
    done.add(wave)


def _delta(qkn, vc, gc3, gt4):
    _, b, s, _ = qkn.shape
    r = DELTA_NCH * DN_CHUNK
    nb = s // r

    def dir_specs(bidx):
        return [
            pl.BlockSpec((1, 1, r, D_DN), lambda bb, i: (0, bb, bidx(i), 0)),
            pl.BlockSpec((1, 1, r, D_DN), lambda bb, i: (1, bb, bidx(i), 0)),
            pl.BlockSpec((1, 1, r, D_DN), lambda bb, i: (0, bb, bidx(i), 0)),
            pl.BlockSpec((1, r, LANES), lambda bb, i: (bb, bidx(i), 0)),
            pl.BlockSpec((1, DELTA_NCH, N_GATE, DN_CHUNK), lambda bb, i: (bb, bidx(i), 0, 0)),
        ]

    up = lambda i: i
    down = lambda i: nb - 1 - i
    return pl.pallas_call(
        _delta_kernel,
        grid=(b, nb),
        in_specs=dir_specs(up) + dir_specs(down),
        out_specs=[
            pl.BlockSpec((1, r, D_DN), lambda bb, i: (bb, up(i), 0)),
            pl.BlockSpec((1, r, D_DN), lambda bb, i: (bb, down(i), 0)),
        ],
        out_shape=[jax.ShapeDtypeStruct((b, s, D_DN), BF16)] * 2,
        scratch_shapes=[pltpu.VMEM((2 * DN_HEADS, DN_HEAD_DIM, DN_HEAD_DIM), F32)],
        compiler_params=pltpu.CompilerParams(
            dimension_semantics=("parallel", "arbitrary"),
            vmem_limit_bytes=VMEM_LIMIT),
        name="delta",
    )(qkn, qkn, vc, gc3, gt4, qkn, qkn, vc, gc3, gt4)


def _outproj_kernel(x_ref, u_ref, v_ref, za_ref, of_ref, ob_ref, zb_ref, lg_ref, lb_ref,
                    ws_ref, bs_ref, nw_ref, w_ref, fw_ref, o_ref, y_ref, vn_ref, *, final):
    _sgu_rows(u_ref, v_ref, za_ref, lg_ref, lb_ref, ws_ref, bs_ref, y_ref, vn_ref)
    for h in range(DN_HEADS):
        cols = slice(h * DN_HEAD_DIM, (h + 1) * DN_HEAD_DIM)
        o = of_ref[:, cols].astype(F32) + ob_ref[:, cols].astype(F32)
        ms = jnp.mean(o * o, axis=-1, keepdims=True)
        z = zb_ref[:, cols].astype(F32)
        yh = o * lax.rsqrt(ms + NORM_EPS) * nw_ref[...] * z
        y_ref[:, D_SGU + h * DN_HEAD_DIM:D_SGU + (h + 1) * DN_HEAD_DIM] = yh.astype(BF16)
    xn = x_ref[...] + jnp.dot(y_ref[...], w_ref[0], preferred_element_type=F32)
    if final:
        ms = jnp.mean(xn * xn, axis=-1, keepdims=True)
        xn = xn * lax.rsqrt(ms + NORM_EPS) * fw_ref[...]
    o_ref[...] = xn


def _outproj(x2, o_f, o_b, proj, ln_g, ln_b, ws_all, bs_t, dn_norm_w, w_out_all, final_w,
             layer, *, final):
    t = x2.shape[0]
    return pl.pallas_call(
        functools.partial(_outproj_kernel, final=final),
        grid=(t // TM_OUT,),
        in_specs=[
            pl.BlockSpec((TM_OUT, D_MODEL), lambda i: (i, 0)),
            pl.BlockSpec((TM_OUT, COL_BLOCK), lambda i: (i, CB_U)),
            pl.BlockSpec((TM_OUT, COL_BLOCK), lambda i: (i, CB_V)),
            pl.BlockSpec((TM_OUT, COL_BLOCK), lambda i: (i, CB_ZA)),
            pl.BlockSpec((TM_OUT, D_DN), lambda i: (i, 0)),
            pl.BlockSpec((TM_OUT, D_DN), lambda i: (i, 0)),
            pl.BlockSpec((TM_OUT, COL_BLOCK), lambda i: (i, CB_ZB)),
            pl.BlockSpec((1, D_SGU), lambda i: (0, 0)),
            pl.BlockSpec((1, D_SGU), lambda i: (0, 0)),
            pl.BlockSpec((1, SGU_GROUPS, SGU_CHUNK, SGU_CHUNK), lambda i: (layer, 0, 0, 0)),
            pl.BlockSpec((SGU_CHUNK, SGU_GROUPS), lambda i: (0, 0)),
            pl.BlockSpec((1, DN_HEAD_DIM), lambda i: (0, 0)),
            pl.BlockSpec((1, D_MODEL, D_MODEL), lambda i: (layer, 0, 0)),
            pl.BlockSpec((1, D_MODEL), lambda i: (0, 0)),
        ],
        out_specs=pl.BlockSpec((TM_OUT, D_MODEL), lambda i: (i, 0)),
        out_shape=jax.ShapeDtypeStruct((t, D_MODEL), F32),
        scratch_shapes=[pltpu.VMEM((TM_OUT, D_MODEL), BF16),
                        pltpu.VMEM((TM_OUT, D_SGU), BF16)],
        compiler_params=pltpu.CompilerParams(
            dimension_semantics=("parallel",), vmem_limit_bytes=VMEM_LIMIT),
        name="outproj_final" if final else "outproj",
    )(x2, proj, proj, proj, o_f, o_b, proj, ln_g, ln_b, ws_all, bs_t, dn_norm_w, w_out_all,
      final_w)


def _cast_kernel(w_ref, o_ref):
    o_ref[...] = w_ref[...].astype(BF16)


def _cast_bf16(w, rows_per_step):
    nl, rows, cols = w.shape
    assert rows % rows_per_step == 0 and rows_per_step % BF16_ROWS == 0
    return pl.pallas_call(
        _cast_kernel,
        grid=(nl, rows // rows_per_step),
        in_specs=[pl.BlockSpec((1, rows_per_step, cols), lambda l, i: (l, i, 0))],
        out_specs=pl.BlockSpec((1, rows_per_step, cols), lambda l, i: (l, i, 0)),
        out_shape=jax.ShapeDtypeStruct(w.shape, BF16),
        compiler_params=pltpu.CompilerParams(
            dimension_semantics=("parallel", "parallel"), vmem_limit_bytes=VMEM_LIMIT),
        name="cast_bf16",
    )(w)


def _pad_lanes(row):
    return jnp.pad(row, (0, LANES - row.shape[0]))[None, :]


def kernel(x, norm_w, w_in, sgu_ln_g, sgu_ln_b, sgu_w, sgu_b, conv_w, a_log_f, a_log_b,
           dt_bias_f, dt_bias_b, dn_norm_w, w_out, final_norm_w):
    b, s, _ = x.shape
    t = b * s
    depth = w_in.shape[0]
    x2 = x.reshape(t, D_MODEL)
    w_t = jnp.swapaxes(w_in, 1, 2)
    w_all = _cast_bf16(w_t, W_IN_CAST_ROWS)
    w_gate = jnp.pad(w_t[:, D_MAIN:, :], ((0, 0), (0, LANES - N_GATE), (0, 0)))
    wg_hi = w_gate.astype(BF16)
    wg_lo = (w_gate - wg_hi.astype(F32)).astype(BF16)
    wg_all = jnp.concatenate([wg_hi, wg_lo], axis=1)
    ws_all = sgu_w.astype(BF16)
    w_out_all = _cast_bf16(w_out, W_OUT_CAST_ROWS)
    for l in range(depth):
        proj, gates_raw = _inproj(x2, norm_w[l][None, :], w_all, wg_all, l)
        alog_row = _pad_lanes(jnp.concatenate([a_log_f[l], a_log_b[l]]))
        dtb_row = _pad_lanes(jnp.concatenate([dt_bias_f[l], dt_bias_b[l]]))
        gc, gt = _gates(gates_raw, alog_row, dtb_row)
        proj3 = proj.reshape(b, s, D_MAIN)
        qkn = _prep(proj3, conv_w[l], first_block=CB_Q, n_blocks=2, normalize=True)
        vc = _prep(proj3, conv_w[l], first_block=CB_Q + 2, n_blocks=1, normalize=False)
        gc3 = gc.reshape(b, s, LANES)
        gt4 = gt.reshape(b, s // DN_CHUNK, N_GATE, DN_CHUNK)
        o_f, o_b = _delta(qkn, vc, gc3, gt4)
        x2 = _outproj(x2, o_f.reshape(t, D_DN), o_b.reshape(t, D_DN), proj,
                      sgu_ln_g[l][None, :], sgu_ln_b[l][None, :], ws_all, sgu_b[l].T,
                      dn_norm_w[l][None, :], w_out_all, final_norm_w[None, :], l,
                      final=(l == depth - 1))
    return x2.reshape(b, s, D_MODEL)
```
